```python
import math
import jax
import jax.numpy as jnp
from jax import lax
import numpy as np

D_MODEL = 1024
BATCH = 8
SEQ = 2048
DEPTH = 4
DEC_BATCH = 128
DEC_SEQ = 1
PAST_LEN = 8192
PAGE_SIZE = 128

Q_BLOCK = 128
MLA_HEADS = 8
MLA_Q_LORA = 256
MLA_KV_LORA = 128
MLA_NOPE = 64
MLA_ROPE = 32
MLA_V = 64
DIFF_HEADS = 4
DIFF_KV_HEADS = 2
DIFF_HALF = 32
DIFF_HEAD_DIM = 2 * DIFF_HALF
FOX_HEADS = 4
FOX_KV_HEADS = 2
FOX_HEAD_DIM = 64
MIX_WIDTH = MLA_HEADS * MLA_V + DIFF_HEADS * DIFF_HEAD_DIM + FOX_HEADS * FOX_HEAD_DIM
IN_SPLITS = (MLA_Q_LORA, MLA_KV_LORA, MLA_ROPE,
             DIFF_HEADS * 2 * DIFF_HALF, DIFF_KV_HEADS * 2 * DIFF_HALF, DIFF_KV_HEADS * DIFF_HEAD_DIM,
             FOX_HEADS * FOX_HEAD_DIM, FOX_KV_HEADS * FOX_HEAD_DIM, FOX_KV_HEADS * FOX_HEAD_DIM, FOX_HEADS)
IN_COLS = sum(IN_SPLITS)
D_FF = 2816
CONV_W = 3
N_MOD = 6
ROPE_THETA = 10000.0
RMS_EPS = 1e-6
NEG_INF = -1e30

kernel_name = 'hybrid_mla_diff_fox_convffn_step'


def rms_norm(x, g):
    x32 = x.astype(jnp.float32)
    y = x32 * lax.rsqrt(jnp.mean(x32 * x32, axis=-1, keepdims=True) + RMS_EPS)
    return (y * g.astype(jnp.float32)).astype(x.dtype)


def rope(x, pos):
    half = x.shape[-1] // 2
    freqs = ROPE_THETA ** (-jnp.arange(half, dtype=jnp.float32) / half)
    ang = pos.astype(jnp.float32)[..., None] * freqs
    ang = ang.reshape(ang.shape[:2] + (1,) * (x.ndim - 3) + (half,))
    cos, sin = jnp.cos(ang), jnp.sin(ang)
    x32 = x.astype(jnp.float32)
    x1, x2 = x32[..., :half], x32[..., half:]
    return jnp.concatenate([x1 * cos - x2 * sin, x2 * cos + x1 * sin], axis=-1).astype(x.dtype)


def causal_mask(qp, k_pos):
    return k_pos[None, None, :] <= qp[:, :, None]


def sweep_query_blocks(fn, q_arrays):
    T = q_arrays[0].shape[1]
    blk = min(Q_BLOCK, T)
    nb = -(-T // blk)
    pad = nb * blk - T

    def to_blocks(a):
        a = jnp.pad(a, [(0, 0), (0, pad)] + [(0, 0)] * (a.ndim - 2), mode='edge')
        a = a.reshape((a.shape[0], nb, blk) + a.shape[2:])
        return jnp.moveaxis(a, 1, 0)

    out = lax.map(fn, tuple(to_blocks(a) for a in q_arrays))
    out = jnp.moveaxis(out, 0, 1)
    out = out.reshape((out.shape[0], nb * blk) + out.shape[3:])
    return out[:, :T]


def gather_past(pool, layer, page_table):
    g = pool[layer, page_table]
    return g.reshape((g.shape[0], g.shape[1] * g.shape[2]) + g.shape[3:])


def token_mixers(h, pos, past, lam_init, lw):
    B, T, _ = h.shape
    f32 = jnp.float32
    cuts = [int(v) for v in np.cumsum(IN_SPLITS)[:-1]]
    (q_lat, kv_lat, k_r, d_q, d_k, d_v, f_q, f_k, f_v, f_z) = jnp.split(h @ lw['w_in'], cuts, axis=-1)

    q = (rms_norm(q_lat, lw['mla_q_norm']) @ lw['mla_w_uq']).reshape(B, T, MLA_HEADS, MLA_NOPE + MLA_ROPE)
    q_pe = rope(q[..., MLA_NOPE:], pos)
    q_abs = jnp.einsum('bthn,chn->bthc', q[..., :MLA_NOPE], lw['mla_w_uk'])
    mla_q = jnp.concatenate([q_abs, q_pe], axis=-1)
    mla_row = jnp.concatenate([rms_norm(kv_lat, lw['mla_kv_norm']), rope(k_r, pos)], axis=-1)

    diff_q = d_q.reshape(B, T, DIFF_KV_HEADS, DIFF_HEADS // DIFF_KV_HEADS, 2, DIFF_HALF)
    diff_row = jnp.stack([d_k.reshape(B, T, DIFF_KV_HEADS, DIFF_HEAD_DIM),
                          d_v.reshape(B, T, DIFF_KV_HEADS, DIFF_HEAD_DIM)], axis=2)

    fox_q = f_q.reshape(B, T, FOX_KV_HEADS, FOX_HEADS // FOX_KV_HEADS, FOX_HEAD_DIM)
    fox_row = jnp.stack([f_k.reshape(B, T, FOX_KV_HEADS, FOX_HEAD_DIM),
                         f_v.reshape(B, T, FOX_KV_HEADS, FOX_HEAD_DIM)], axis=2)
    logf_row = jax.nn.log_sigmoid((f_z + lw['fox_b_f']).astype(f32)).astype(h.dtype)

    rows = (mla_row, diff_row, fox_row, logf_row)
    if past is None:
        mla_all, diff_all, fox_all, logf_all = rows
    else:
        mla_all, diff_all, fox_all, logf_all = [jnp.concatenate([p.astype(r.dtype), r], axis=1)
                                                for p, r in zip(past, rows)]
    Tk = mla_all.shape[1]
    k_pos = jnp.arange(Tk, dtype=jnp.int32)

    mla_val = mla_all[..., :MLA_KV_LORA]
    mla_scale = 1.0 / math.sqrt(MLA_NOPE + MLA_ROPE)

    def mla_block(args):
        qb, qp = args
        s = jnp.einsum('bqhe,bke->bhqk', qb, mla_all).astype(f32) * mla_scale
        s = jnp.where(causal_mask(qp, k_pos)[:, None], s, NEG_INF)
        p = jax.nn.softmax(s, axis=-1).astype(mla_val.dtype)
        return jnp.einsum('bhqk,bkc->bqhc', p, mla_val)

    o_lat = sweep_query_blocks(mla_block, (mla_q, pos))
    mla_out = jnp.einsum('bthc,chv->bthv', o_lat, lw['mla_w_uv']).reshape(B, T, MLA_HEADS * MLA_V)

    diff_k = diff_all[:, :, 0].reshape(B, Tk, DIFF_KV_HEADS, 2, DIFF_HALF)
    diff_v = diff_all[:, :, 1]
    slopes = (2.0 ** (-8.0 * jnp.arange(1, DIFF_HEADS + 1, dtype=f32) / DIFF_HEADS)).reshape(DIFF_KV_HEADS, -1)
    lv = lw['diff_lambda'].astype(f32)
    lam = jnp.exp(jnp.sum(lv[0] * lv[1])) - jnp.exp(jnp.sum(lv[2] * lv[3])) + lam_init
    diff_scale = 1.0 / math.sqrt(DIFF_HALF)

    def diff_block(args):
        qb, qp = args
        s = jnp.einsum('bqgrid,bkgid->bgriqk', qb, diff_k).astype(f32) * diff_scale
        dist = (qp[:, :, None] - k_pos[None, None, :]).astype(f32)
        s = s - slopes[None, :, :, None, None, None] * dist[:, None, None, None]
        s = jnp.where(causal_mask(qp, k_pos)[:, None, None, None], s, NEG_INF)
        p = jax.nn.softmax(s, axis=-1)
        pd = (p[:, :, :, 0] - lam * p[:, :, :, 1]).astype(diff_v.dtype)
        return jnp.einsum('bgrqk,bkgd->bqgrd', pd, diff_v)

    o_diff = sweep_query_blocks(diff_block, (diff_q, pos)).reshape(B, T, DIFF_HEADS, DIFF_HEAD_DIM)
    diff_out = (rms_norm(o_diff, lw['diff_subln']) * (1.0 - lam_init)).reshape(B, T, DIFF_HEADS * DIFF_HEAD_DIM)

    fox_k = fox_all[:, :, 0]
    fox_v = fox_all[:, :, 1]
    F = jnp.cumsum(logf_all.astype(f32), axis=1).reshape(B, Tk, FOX_KV_HEADS, -1)
    F_q = F[:, Tk - T:]
    F_k = jnp.moveaxis(F, 1, -1)
    fox_scale = 1.0 / math.sqrt(FOX_HEAD_DIM)

    def fox_block(args):
        qb, fq, qp = args
        s = jnp.einsum('bqgrd,bkgd->bgrqk', qb, fox_k).astype(f32) * fox_scale
        s = s + jnp.moveaxis(fq, 1, -1)[..., None] - F_k[:, :, :, None, :]
        s = jnp.where(causal_mask(qp, k_pos)[:, None, None], s, NEG_INF)
        p = jax.nn.softmax(s, axis=-1).astype(fox_v.dtype)
        return jnp.einsum('bgrqk,bkgd->bqgrd', p, fox_v)

    fox_out = sweep_query_blocks(fox_block, (fox_q, F_q, pos)).reshape(B, T, FOX_HEADS * FOX_HEAD_DIM)

    mix = jnp.concatenate([mla_out, diff_out, fox_out], axis=-1) @ lw['w_o']
    return mix, rows


def conv_ffn(h, conv_prev, lw):
    T = h.shape[1]
    a, g = jnp.split(h @ lw['ffn_w_up'], 2, axis=-1)
    ext = jnp.concatenate([conv_prev.astype(a.dtype), a], axis=1)
    w = lw['ffn_conv_w']
    conv = lw['ffn_conv_b'] + sum(w[k] * ext[:, k:k + T] for k in range(CONV_W))
    y = (jax.nn.gelu(conv, approximate=True) * g) @ lw['ffn_w_down']
    return y, ext[:, ext.shape[1] - (CONV_W - 1):]


def decoder_layer(x, c, pos, past, conv_prev, lam_init, lw):
    mod = (jax.nn.silu(c) @ lw['w_ada'] + lw['b_ada'])[:, None, :]
    shift_a, scale_a, gate_a, shift_f, scale_f, gate_f = jnp.split(mod, N_MOD, axis=-1)
    h = rms_norm(x, lw['g_attn_pre']) * (1.0 + scale_a) + shift_a
    mix, rows = token_mixers(h, pos, past, lam_init, lw)
    x = x + gate_a * rms_norm(mix, lw['g_attn_post'])
    h = rms_norm(x, lw['g_ffn_pre']) * (1.0 + scale_f) + shift_f
    ff, conv_new = conv_ffn(h, conv_prev, lw)
    x = x + gate_f * rms_norm(ff, lw['g_ffn_post'])
    return x, rows + (conv_new,)


def setup_inputs(seed: int = 0) -> dict:
    key = jax.random.key(seed)
    ks = iter(jax.random.split(key, 40))
    f32 = jnp.float32

    def nrm(shape, scale=1.0):
        return jax.random.normal(next(ks), shape, f32) * scale

    def gain(shape):
        return 1.0 + nrm(shape, 0.05)

    L = DEPTH
    n_pages = PAST_LEN // PAGE_SIZE
    n_used = DEC_BATCH * n_pages
    n_pool = n_used + max(1, n_used // 4)
    x_prompt = nrm((BATCH, SEQ, D_MODEL))
    x_sample = nrm((DEC_BATCH, DEC_SEQ, D_MODEL))
    c_prompt = nrm((BATCH, D_MODEL))
    c_sample = nrm((DEC_BATCH, D_MODEL))
    cache_mla = nrm((L, n_pool, PAGE_SIZE, MLA_KV_LORA + MLA_ROPE))
    cache_diff_kv = nrm((L, n_pool, PAGE_SIZE, 2, DIFF_KV_HEADS, DIFF_HEAD_DIM))
    cache_fox_kv = nrm((L, n_pool, PAGE_SIZE, 2, FOX_KV_HEADS, FOX_HEAD_DIM))
    cache_fox_logf = jax.nn.log_sigmoid(1.0 + nrm((L, n_pool, PAGE_SIZE, FOX_HEADS)))
    state_conv = nrm((L, DEC_BATCH, CONV_W - 1, D_FF))
    page_table = jax.random.permutation(next(ks), n_pool)[:n_used].reshape(DEC_BATCH, n_pages).astype(jnp.int32)
    return {
        'x_prompt': x_prompt,
        'x_sample': x_sample,
        'c_prompt': c_prompt,
        'c_sample': c_sample,
        'cache_mla': cache_mla,
        'cache_diff_kv': cache_diff_kv,
        'cache_fox_kv': cache_fox_kv,
        'cache_fox_logf': cache_fox_logf,
        'state_conv': state_conv,
        'page_table': page_table,
        'w_ada': nrm((L, D_MODEL, N_MOD * D_MODEL), 0.5 * D_MODEL ** -0.5),
        'b_ada': nrm((L, N_MOD * D_MODEL), 0.01),
        'g_attn_pre': gain((L, D_MODEL)),
        'g_attn_post': gain((L, D_MODEL)),
        'g_ffn_pre': gain((L, D_MODEL)),
        'g_ffn_post': gain((L, D_MODEL)),
        'w_in': nrm((L, D_MODEL, IN_COLS), D_MODEL ** -0.5),
        'mla_q_norm': gain((L, MLA_Q_LORA)),
        'mla_kv_norm': gain((L, MLA_KV_LORA)),
        'mla_w_uq': nrm((L, MLA_Q_LORA, MLA_HEADS * (MLA_NOPE + MLA_ROPE)), MLA_Q_LORA ** -0.5),
        'mla_w_uk': nrm((L, MLA_KV_LORA, MLA_HEADS, MLA_NOPE), MLA_KV_LORA ** -0.5),
        'mla_w_uv': nrm((L, MLA_KV_LORA, MLA_HEADS, MLA_V), MLA_KV_LORA ** -0.5),
        'diff_lambda': nrm((L, 4, DIFF_HALF), 0.1),
        'diff_subln': gain((L, DIFF_HEADS, DIFF_HEAD_DIM)),
        'fox_b_f': 1.0 + nrm((L, FOX_HEADS), 0.1),
        'w_o': nrm((L, MIX_WIDTH, D_MODEL), MIX_WIDTH ** -0.5),
        'ffn_w_up': nrm((L, D_MODEL, 2 * D_FF), D_MODEL ** -0.5),
        'ffn_conv_w': nrm((L, CONV_W, D_FF), 0.5),
        'ffn_conv_b': nrm((L, D_FF), 0.01),
        'ffn_w_down': nrm((L, D_FF, D_MODEL), D_FF ** -0.5),
    }


def reference(x_prompt, x_sample, c_prompt, c_sample, cache_mla, cache_diff_kv, cache_fox_kv,
              cache_fox_logf, state_conv, page_table, w_ada, b_ada, g_attn_pre, g_attn_post,
              g_ffn_pre, g_ffn_post, w_in, mla_q_norm, mla_kv_norm, mla_w_uq, mla_w_uk, mla_w_uv,
              diff_lambda, diff_subln, fox_b_f, w_o, ffn_w_up, ffn_conv_w, ffn_conv_b, ffn_w_down):
    B, T, _ = x_prompt.shape
    DB, TS, _ = x_sample.shape
    past_len = page_table.shape[1] * cache_mla.shape[2]
    pos_p = jnp.broadcast_to(jnp.arange(T, dtype=jnp.int32)[None], (B, T))
    pos_s = jnp.broadcast_to(past_len + jnp.arange(TS, dtype=jnp.int32)[None], (DB, TS))
    conv0 = jnp.zeros((B, CONV_W - 1, D_FF), x_prompt.dtype)
    xp, xs = x_prompt, x_sample
    rows_p, rows_s = [], []
    for l in range(DEPTH):
        lw = {
            'w_ada': w_ada[l], 'b_ada': b_ada[l],
            'g_attn_pre': g_attn_pre[l], 'g_attn_post': g_attn_post[l],
            'g_ffn_pre': g_ffn_pre[l], 'g_ffn_post': g_ffn_post[l],
            'w_in': w_in[l], 'mla_q_norm': mla_q_norm[l], 'mla_kv_norm': mla_kv_norm[l],
            'mla_w_uq': mla_w_uq[l], 'mla_w_uk': mla_w_uk[l], 'mla_w_uv': mla_w_uv[l],
            'diff_lambda': diff_lambda[l], 'diff_subln': diff_subln[l], 'fox_b_f': fox_b_f[l],
            'w_o': w_o[l], 'ffn_w_up': ffn_w_up[l], 'ffn_conv_w': ffn_conv_w[l],
            'ffn_conv_b': ffn_conv_b[l], 'ffn_w_down': ffn_w_down[l],
        }
        lam_init = 0.8 - 0.6 * math.exp(-0.3 * l)
        xp, rp = decoder_layer(xp, c_prompt, pos_p, None, conv0, lam_init, lw)
        past = (gather_past(cache_mla, l, page_table), gather_past(cache_diff_kv, l, page_table),
                gather_past(cache_fox_kv, l, page_table), gather_past(cache_fox_logf, l, page_table))
        xs, rs = decoder_layer(xs, c_sample, pos_s, past, state_conv[l], lam_init, lw)
        rows_p.append(rp)
        rows_s.append(rs)
    new_mla_p = jnp.stack([r[0] for r in rows_p])
    new_mla_s = jnp.stack([r[0] for r in rows_s])
    new_diff_p = jnp.stack([r[1] for r in rows_p])
    new_diff_s = jnp.stack([r[1] for r in rows_s])
    new_foxkv_p = jnp.stack([r[2] for r in rows_p])
    new_foxkv_s = jnp.stack([r[2] for r in rows_s])
    new_logf_p = jnp.stack([r[3] for r in rows_p])
    new_logf_s = jnp.stack([r[3] for r in rows_s])
    new_conv_p = jnp.stack([r[4] for r in rows_p])
    new_conv_s = jnp.stack([r[4] for r in rows_s])
    return (xp, xs, new_mla_p, new_mla_s, new_diff_p, new_diff_s, new_foxkv_p, new_foxkv_s,
            new_logf_p, new_logf_s, new_conv_p, new_conv_s)
```

```python
import functools
import math

import jax
import jax.numpy as jnp
import numpy as np
from jax import lax
from jax.experimental import pallas as pl
from jax.experimental.pallas import tpu as pltpu

F32 = jnp.float32
BF16 = jnp.bfloat16

D_MODEL = 1024
D_FF = 2816
N_LAYERS = 4
MLA_HEADS = 8
MLA_Q_LORA = 256
MLA_KV_LORA = 128
MLA_NOPE = 64
MLA_ROPE = 32
MLA_V = 64
MLA_ROW = MLA_KV_LORA + MLA_ROPE
DIFF_HEADS = 4
DIFF_HALF = 32
FOX_HEADS = 4
HEAD_DIM = 64
ROPE_THETA = 10000.0
RMS_EPS = 1e-6
NEG_INF = -1e30
PAGE = 128

C_QLAT = 0
C_KVLAT = 256
C_KR = 384
C_KRS = 512
C_FZ = 640
C_DQ = 768
C_DROW = 1792
C_FQ = 2048
C_FROW = 2560
C_TOTAL = 2816

FFN_CHUNK = 256
FFN_NCHUNK = D_FF // FFN_CHUNK

PRE_TM = 512
POST_TM = 512
ATT_TQ = 128
DEC_PAGES = 8
VMEM_LIMIT = 56 * 1024 * 1024


def _cparams(sem):
    return pltpu.CompilerParams(dimension_semantics=sem, vmem_limit_bytes=VMEM_LIMIT)


def _const_spec(shape, index_map):
    return pl.BlockSpec(shape, index_map, pipeline_mode=pl.Buffered(1))


def _rms(x, g):
    return x * lax.rsqrt(jnp.mean(x * x, axis=-1, keepdims=True) + RMS_EPS) * g


def _dot(a, b):
    return jnp.dot(a, b, preferred_element_type=F32)


def _dot_nt(a, b):
    return lax.dot_general(a, b, (((1,), (1,)), ((), ())), preferred_element_type=F32)


def _ada_kernel(c_ref, w_ref, b_ref, o_ref):
    c = c_ref[...]
    s = (c * jax.nn.sigmoid(c)).astype(BF16)
    o_ref[...] = _dot(s, w_ref[...].astype(BF16)) + b_ref[...]


def _ada_call(c_all, w_ada, b_ada):
    n = c_all.shape[0]
    tn = 1536
    return pl.pallas_call(
        _ada_kernel,
        grid=(N_LAYERS, 6 * D_MODEL // tn),
        in_specs=[pl.BlockSpec((n, D_MODEL), lambda l, j: (0, 0)),
                  pl.BlockSpec((None, D_MODEL, tn), lambda l, j: (l, 0, j)),
                  pl.BlockSpec((None, 1, tn), lambda l, j: (l, 0, j))],
        out_specs=pl.BlockSpec((None, n, tn), lambda l, j: (l, 0, j)),
        out_shape=jax.ShapeDtypeStruct((N_LAYERS, n, 6 * D_MODEL), F32),
        compiler_params=_cparams(("arbitrary", "arbitrary")),
        name="ada",
    )(c_all, w_ada, b_ada.reshape(N_LAYERS, 1, 6 * D_MODEL))


def _pre_kernel(x_ref, shift_ref, scale_ref, g_ref, win_ref, qn_ref, kvn_ref, wqn_ref, wqp_ref, wqps_ref,
                wuk_ref, bf_ref, cq_ref, sq_ref, ck_ref, sk_ref,
                mq_ref, mrow_ref, dq_ref, drow_ref, fq_ref, frow_ref, logf_ref):
    x = x_ref[...]
    h = _rms(x, g_ref[...]) * (1.0 + scale_ref[...]) + shift_ref[...]
    z = _dot(h.astype(BF16), win_ref[...])

    qn = _rms(z[:, C_QLAT:C_QLAT + MLA_Q_LORA], qn_ref[...]).astype(BF16)
    q_nope = _dot(qn, wqn_ref[...])
    q_pe = _dot(qn, wqp_ref[...]) * cq_ref[...] + _dot(qn, wqps_ref[...]) * sq_ref[...]
    q_abs = _dot(q_nope.astype(BF16), wuk_ref[...])
    for hh in range(MLA_HEADS):
        mq_ref[hh, :, 0:MLA_KV_LORA] = q_abs[:, hh * 128:(hh + 1) * 128].astype(BF16)
        mq_ref[hh, :, MLA_KV_LORA:MLA_ROW] = q_pe[:, hh * MLA_ROPE:(hh + 1) * MLA_ROPE].astype(BF16)

    mrow_ref[:, 0:MLA_KV_LORA] = _rms(z[:, C_KVLAT:C_KVLAT + MLA_KV_LORA], kvn_ref[...])
    mrow_ref[:, MLA_KV_LORA:MLA_ROW] = (z[:, C_KR:C_KR + MLA_ROPE] * ck_ref[...]
                                        + z[:, C_KRS:C_KRS + MLA_ROPE] * sk_ref[...])

    for j in range(8):
        dq_ref[j] = z[:, C_DQ + j * 128:C_DQ + (j + 1) * 128].astype(BF16)
    drow_ref[...] = z[:, C_DROW:C_DROW + 256]
    for j in range(4):
        fq_ref[j] = z[:, C_FQ + j * 128:C_FQ + (j + 1) * 128].astype(BF16)
    frow_ref[...] = z[:, C_FROW:C_FROW + 256]

    fz = z[:, C_FZ:C_FZ + FOX_HEADS] + bf_ref[...]
    logf_ref[...] = jnp.minimum(fz, 0.0) - jnp.log1p(jnp.exp(-jnp.abs(fz)))


def _pre_call(x, mod, lw, tabs, tm):
    B, T, _ = x.shape
    R = mod.shape[1]
    rb = 1 if R == 1 else tm
    mod_map = (lambda c: (lambda b, t: (b, 0, c))) if R == 1 else (lambda c: (lambda b, t: (b, t, c)))
    cq, sq, ck, sk = tabs
    w2 = lambda shape: _const_spec(shape, lambda b, t: (0, 0))
    row = lambda n: pl.BlockSpec((tm, n), lambda b, t: (t, 0))
    outs = pl.pallas_call(
        _pre_kernel,
        grid=(B, T // tm),
        in_specs=[pl.BlockSpec((None, tm, D_MODEL), lambda b, t: (b, t, 0)),
                  pl.BlockSpec((None, rb, D_MODEL), mod_map(0)),
                  pl.BlockSpec((None, rb, D_MODEL), mod_map(1)),
                  w2((1, D_MODEL)), w2((D_MODEL, C_TOTAL)), w2((1, MLA_Q_LORA)), w2((1, MLA_KV_LORA)),
                  w2((MLA_Q_LORA, 512)), w2((MLA_Q_LORA, 256)), w2((MLA_Q_LORA, 256)), w2((512, 1024)),
                  w2((1, FOX_HEADS)), row(256), row(256), row(MLA_ROPE), row(MLA_ROPE)],
        out_specs=[pl.BlockSpec((None, 8, tm, MLA_ROW), lambda b, t: (b, 0, t, 0)),
                   pl.BlockSpec((None, tm, MLA_ROW), lambda b, t: (b, t, 0)),
                   pl.BlockSpec((None, 8, tm, 128), lambda b, t: (b, 0, t, 0)),
                   pl.BlockSpec((None, tm, 256), lambda b, t: (b, t, 0)),
                   pl.BlockSpec((None, 4, tm, 128), lambda b, t: (b, 0, t, 0)),
                   pl.BlockSpec((None, tm, 256), lambda b, t: (b, t, 0)),
                   pl.BlockSpec((None, tm, FOX_HEADS), lambda b, t: (b, t, 0))],
        out_shape=[jax.ShapeDtypeStruct((B, 8, T, MLA_ROW), BF16),
                   jax.ShapeDtypeStruct((B, T, MLA_ROW), F32),
                   jax.ShapeDtypeStruct((B, 8, T, 128), BF16),
                   jax.ShapeDtypeStruct((B, T, 256), F32),
                   jax.ShapeDtypeStruct((B, 4, T, 128), BF16),
                   jax.ShapeDtypeStruct((B, T, 256), F32),
                   jax.ShapeDtypeStruct((B, T, FOX_HEADS), F32)],
        compiler_params=_cparams(("arbitrary", "arbitrary")),
        name="pre",
    )(x, mod, mod, lw["g_attn_pre"], lw["win"], lw["mla_q_norm"], lw["mla_kv_norm"], lw["wq_nope"],
      lw["wq_pe"], lw["wq_pe_sw"], lw["wuk_bd"], lw["fox_b_f"], cq, sq, ck, sk)
    return outs


def _online_update(s, v_bf, m_sc, l_sc, acc_sc):
    m_prev = m_sc[...]
    m_new = jnp.maximum(m_prev, jnp.max(s, axis=-1, keepdims=True))
    alpha = jnp.exp(m_prev - m_new)
    p = jnp.exp(s - m_new)
    l_sc[...] = alpha * l_sc[...] + jnp.sum(p, axis=-1, keepdims=True)
    acc_sc[...] = alpha * acc_sc[...] + _dot(p.astype(BF16), v_bf)
    m_sc[...] = m_new


def _init_softmax(m_sc, l_sc, acc_sc):
    m_sc[...] = jnp.full(m_sc.shape, NEG_INF, F32)
    l_sc[...] = jnp.zeros(l_sc.shape, F32)
    acc_sc[...] = jnp.zeros(acc_sc.shape, F32)


def _diag_mask(nblk, tq):
    rows = lax.broadcasted_iota(jnp.int32, (nblk * tq, tq), 0) & (tq - 1)
    cols = lax.broadcasted_iota(jnp.int32, (nblk * tq, tq), 1)
    return cols <= rows


def _alibi_slope(head):
    return jnp.where(head == 0, 2.0 ** -2, jnp.where(head == 1, 2.0 ** -4,
                                                     jnp.where(head == 2, 2.0 ** -6, 2.0 ** -8))).astype(F32)


def _lambda_value(lam_ref, lam_init):
    lv = lam_ref[...]
    a = jnp.sum(lv[0:1, :] * lv[1:2, :], axis=-1, keepdims=True)
    b = jnp.sum(lv[2:3, :] * lv[3:4, :], axis=-1, keepdims=True)
    return jnp.exp(a) - jnp.exp(b) + lam_init


def _diff_combine(blocks, lam, subln, lam_init):
    outs = []
    for g in range(2):
        for r in range(2):
            o = blocks[g * 4 + r * 2] - lam * blocks[g * 4 + r * 2 + 1]
            o = o[:, g * HEAD_DIM:(g + 1) * HEAD_DIM]
            hh = g * 2 + r
            outs.append(_rms(o, subln[:, hh * HEAD_DIM:(hh + 1) * HEAD_DIM]) * (1.0 - lam_init))
    return jnp.concatenate(outs, axis=1)


def _fox_select(blocks):
    return jnp.concatenate([blocks[g * 2 + r][:, g * HEAD_DIM:(g + 1) * HEAD_DIM]
                            for g in range(2) for r in range(2)], axis=1)


def _mla_attn_kernel(q_ref, kv_ref, o_ref, m_sc, l_sc, acc_sc):
    tq = ATT_TQ
    qi = pl.program_id(1)
    q = q_ref[...].reshape(MLA_HEADS * tq, MLA_ROW)
    scale = 1.0 / math.sqrt(MLA_NOPE + MLA_ROPE)
    _init_softmax(m_sc, l_sc, acc_sc)

    def step(kt, masked):
        kb = kv_ref[pl.ds(pl.multiple_of(kt * tq, tq), tq), :].astype(BF16)
        s = _dot_nt(q, kb) * scale
        if masked:
            s = jnp.where(_diag_mask(MLA_HEADS, tq), s, NEG_INF)
        _online_update(s, kb[:, 0:MLA_KV_LORA], m_sc, l_sc, acc_sc)

    def body(kt, c):
        step(kt, False)
        return c

    lax.fori_loop(0, qi, body, 0)
    step(qi, True)
    o = acc_sc[...] / l_sc[...]
    for hh in range(MLA_HEADS):
        o_ref[:, hh * 128:(hh + 1) * 128] = o[hh * tq:(hh + 1) * tq].astype(BF16)


def _mla_attn_call(mq, mrow):
    B, _, T, _ = mq.shape
    tq = ATT_TQ
    return pl.pallas_call(
        _mla_attn_kernel,
        grid=(B, T // tq),
        in_specs=[pl.BlockSpec((None, MLA_HEADS, tq, MLA_ROW), lambda b, i: (b, 0, i, 0)),
                  pl.BlockSpec((None, T, MLA_ROW), lambda b, i: (b, 0, 0))],
        out_specs=pl.BlockSpec((None, tq, MLA_HEADS * 128), lambda b, i: (b, i, 0)),
        out_shape=jax.ShapeDtypeStruct((B, T, MLA_HEADS * 128), BF16),
        scratch_shapes=[pltpu.VMEM((MLA_HEADS * tq, 1), F32), pltpu.VMEM((MLA_HEADS * tq, 1), F32),
                        pltpu.VMEM((MLA_HEADS * tq, 128), F32)],
        compiler_params=_cparams(("arbitrary", "arbitrary")),
        name="mla_attn",
    )(mq, mrow)


def _diff_attn_kernel(lam_init, q_ref, kv_ref, lam_ref, subln_ref, o_ref, m_sc, l_sc, acc_sc, sb_sc, sl_sc):
    tq = ATT_TQ
    qi = pl.program_id(1)
    q = q_ref[...].reshape(8 * tq, 128)
    scale = 1.0 / math.sqrt(DIFF_HALF)
    _init_softmax(m_sc, l_sc, acc_sc)
    rows = lax.broadcasted_iota(jnp.int32, (8 * tq, tq), 0)
    cols = lax.broadcasted_iota(jnp.int32, (8 * tq, tq), 1)
    slope = _alibi_slope(rows >> (int(math.log2(tq)) + 1))
    sb_sc[...] = slope * ((rows & (tq - 1)) - cols).astype(F32)
    sl_sc[...] = slope[:, 0:1]

    def step(kt, masked):
        k0 = pl.multiple_of(kt * tq, tq)
        kb = kv_ref[pl.ds(k0, tq), 0:128].astype(BF16)
        vb = kv_ref[pl.ds(k0, tq), 128:256].astype(BF16)
        off = ((qi - kt) * tq).astype(F32)
        s = _dot_nt(q, kb) * scale - sb_sc[...] - sl_sc[...] * off
        if masked:
            s = jnp.where(_diag_mask(8, tq), s, NEG_INF)
        _online_update(s, vb, m_sc, l_sc, acc_sc)

    def body(kt, c):
        step(kt, False)
        return c

    lax.fori_loop(0, qi, body, 0)
    step(qi, True)
    o = acc_sc[...] / l_sc[...]
    lam = _lambda_value(lam_ref, lam_init)
    blocks = [o[j * tq:(j + 1) * tq] for j in range(8)]
    o_ref[...] = _diff_combine(blocks, lam, subln_ref[...], lam_init).astype(BF16)


def _diff_attn_call(dq, drow, lam_p, subln, lam_init):
    B, _, T, _ = dq.shape
    tq = ATT_TQ
    return pl.pallas_call(
        functools.partial(_diff_attn_kernel, lam_init),
        grid=(B, T // tq),
        in_specs=[pl.BlockSpec((None, 8, tq, 128), lambda b, i: (b, 0, i, 0)),
                  pl.BlockSpec((None, T, 256), lambda b, i: (b, 0, 0)),
                  pl.BlockSpec((4, DIFF_HALF), lambda b, i: (0, 0)),
                  pl.BlockSpec((1, 256), lambda b, i: (0, 0))],
        out_specs=pl.BlockSpec((None, tq, 256), lambda b, i: (b, i, 0)),
        out_shape=jax.ShapeDtypeStruct((B, T, 256), BF16),
        scratch_shapes=[pltpu.VMEM((8 * tq, 1), F32), pltpu.VMEM((8 * tq, 1), F32),
                        pltpu.VMEM((8 * tq, 128), F32), pltpu.VMEM((8 * tq, tq), F32),
                        pltpu.VMEM((8 * tq, 1), F32)],
        compiler_params=_cparams(("arbitrary", "arbitrary")),
        name="diff_attn",
    )(dq, drow, lam_p, subln)


def _fox_attn_kernel(q_ref, kv_ref, fk_ref, fq_ref, o_ref, m_sc, l_sc, acc_sc):
    tq = ATT_TQ
    qi = pl.program_id(1)
    q = q_ref[...].reshape(4 * tq, 128)
    scale = 1.0 / math.sqrt(HEAD_DIM)
    _init_softmax(m_sc, l_sc, acc_sc)
    fq = fq_ref[...]
    fq_col = jnp.concatenate([fq[:, hh:hh + 1] for hh in range(FOX_HEADS)], axis=0)

    def step(kt, masked):
        k0 = pl.multiple_of(kt * tq, tq)
        kb = kv_ref[pl.ds(k0, tq), 0:128].astype(BF16)
        vb = kv_ref[pl.ds(k0, tq), 128:256].astype(BF16)
        fk = fk_ref[kt]
        fk_full = jnp.concatenate([jnp.broadcast_to(fk[hh:hh + 1, :], (tq, tq)) for hh in range(FOX_HEADS)],
                                  axis=0)
        s = (_dot_nt(q, kb) * scale + fq_col) - fk_full
        if masked:
            s = jnp.where(_diag_mask(4, tq), s, NEG_INF)
        _online_update(s, vb, m_sc, l_sc, acc_sc)

    def body(kt, c):
        step(kt, False)
        return c

    lax.fori_loop(0, qi, body, 0)
    step(qi, True)
    o = acc_sc[...] / l_sc[...]
    o_ref[...] = _fox_select([o[j * tq:(j + 1) * tq] for j in range(4)]).astype(BF16)


def _fox_attn_call(fq, frow, f_cum, f_cum_t):
    B, _, T, _ = fq.shape
    tq = ATT_TQ
    return pl.pallas_call(
        _fox_attn_kernel,
        grid=(B, T // tq),
        in_specs=[pl.BlockSpec((None, 4, tq, 128), lambda b, i: (b, 0, i, 0)),
                  pl.BlockSpec((None, T, 256), lambda b, i: (b, 0, 0)),
                  pl.BlockSpec((None, T // tq, 8, tq), lambda b, i: (b, 0, 0, 0)),
                  pl.BlockSpec((None, tq, FOX_HEADS), lambda b, i: (b, i, 0))],
        out_specs=pl.BlockSpec((None, tq, 256), lambda b, i: (b, i, 0)),
        out_shape=jax.ShapeDtypeStruct((B, T, 256), BF16),
        scratch_shapes=[pltpu.VMEM((4 * tq, 1), F32), pltpu.VMEM((4 * tq, 1), F32),
                        pltpu.VMEM((4 * tq, 128), F32)],
        compiler_params=_cparams(("arbitrary", "arbitrary")),
        name="fox_attn",
    )(fq, frow, f_cum, f_cum_t)


def _split3(x):
    hi = x.astype(BF16)
    r = x - hi.astype(F32)
    mid = r.astype(BF16)
    lo = (r - mid.astype(F32)).astype(BF16)
    return hi, mid, lo


def _cumsum_kernel(x_ref, tri_ref, o_ref):
    hi, mid, lo = _split3(x_ref[...])
    tri = tri_ref[...]
    o_ref[...] = (_dot(hi, tri) + _dot(mid, tri)) + _dot(lo, tri)


def _cumsum_call(x, tri):
    B, R, T = x.shape
    return pl.pallas_call(
        _cumsum_kernel,
        grid=(B,),
        in_specs=[pl.BlockSpec((None, R, T), lambda b: (b, 0, 0)),
                  _const_spec((T, T), lambda b: (0, 0))],
        out_specs=pl.BlockSpec((None, R, T), lambda b: (b, 0, 0)),
        out_shape=jax.ShapeDtypeStruct((B, R, T), F32),
        compiler_params=_cparams(("arbitrary",)),
        name="logf_cumsum",
    )(x, tri)


def _decode_kernel(past_len, pt_ref, mq_ref, dq_ref, fq_ref, mnew_ref, dnew_ref, fnew_ref, lfnew_ref, triu_ref,
                   *rest):
    np_ = DEC_PAGES
    mla_pages = rest[0:np_]
    diff_pages = rest[np_:2 * np_]
    fox_pages = rest[2 * np_:3 * np_]
    logf_pages = rest[3 * np_:4 * np_]
    mo_ref, do_ref, fo_ref = rest[4 * np_:4 * np_ + 3]
    (mm, ml, macc, dm, dl, dacc, fm, fl, facc, carry) = rest[4 * np_ + 3:]
    c = pl.program_id(1)
    nch = pl.num_programs(1)
    width = np_ * PAGE

    mq = mq_ref[...]
    dq = dq_ref[...]
    fq = fq_ref[...]
    m_scale = 1.0 / math.sqrt(MLA_NOPE + MLA_ROPE)
    d_scale = 1.0 / math.sqrt(DIFF_HALF)
    f_scale = 1.0 / math.sqrt(HEAD_DIM)

    @pl.when(c == 0)
    def _():
        def self_init(q, k_new, v_new, scale, m_sc, l_sc, acc_sc):
            kb = k_new.astype(BF16).astype(F32)
            m_sc[...] = jnp.sum(q.astype(F32) * kb, axis=-1, keepdims=True) * scale
            l_sc[...] = jnp.ones(l_sc.shape, F32)
            acc_sc[...] = jnp.broadcast_to(v_new.astype(BF16).astype(F32), acc_sc.shape)

        mnew = mnew_ref[...]
        self_init(mq, mnew, mnew[:, 0:MLA_KV_LORA], m_scale, mm, ml, macc)
        dnew = dnew_ref[...]
        self_init(dq, dnew[:, 0:128], dnew[:, 128:256], d_scale, dm, dl, dacc)
        fnew = fnew_ref[...]
        self_init(fq, fnew[:, 0:128], fnew[:, 128:256], f_scale, fm, fl, facc)
        carry[...] = lfnew_ref[...]

    def update(s, vt_bf, m_sc, l_sc, acc_sc):
        m_prev = m_sc[...]
        m_new = jnp.maximum(m_prev, jnp.max(s, axis=-1, keepdims=True))
        alpha = jnp.exp(m_prev - m_new)
        p = jnp.exp(s - m_new)
        l_sc[...] = alpha * l_sc[...] + jnp.sum(p, axis=-1, keepdims=True)
        acc_sc[...] = alpha * acc_sc[...] + _dot_nt(p.astype(BF16), vt_bf)
        m_sc[...] = m_new

    mla_t = jnp.concatenate([r[...].astype(BF16) for r in mla_pages], axis=1)
    update(_dot(mq, mla_t) * m_scale, mla_t[0:MLA_KV_LORA, :], mm, ml, macc)

    first_pos = (nch - 1 - c) * width
    diff_k = jnp.concatenate([r[0:128, :].astype(BF16) for r in diff_pages], axis=1)
    diff_v = jnp.concatenate([r[128:256, :].astype(BF16) for r in diff_pages], axis=1)
    kpos = first_pos + lax.broadcasted_iota(jnp.int32, (8, width), 1)
    slope = _alibi_slope(lax.broadcasted_iota(jnp.int32, (8, width), 0) >> 1)
    s_d = _dot(dq, diff_k) * d_scale - slope * (past_len - kpos).astype(F32)
    update(s_d, diff_v, dm, dl, dacc)

    fox_k = jnp.concatenate([r[0:128, :].astype(BF16) for r in fox_pages], axis=1)
    fox_v = jnp.concatenate([r[128:256, :].astype(BF16) for r in fox_pages], axis=1)
    lf = [r[...] for r in logf_pages]
    hi, mid, lo = _split3(jnp.concatenate(lf, axis=0))
    triu = triu_ref[...]
    within = (_dot(hi, triu) + _dot(mid, triu)) + _dot(lo, triu)
    run = carry[...]
    bias = [None] * np_
    for j in range(np_ - 1, -1, -1):
        bias[j] = within[j * 8:(j + 1) * 8] + run
        run = run + jnp.sum(lf[j], axis=-1, keepdims=True)
    carry[...] = run
    s_f = _dot(fq, fox_k) * f_scale + jnp.concatenate(bias, axis=1)
    update(s_f, fox_v, fm, fl, facc)

    @pl.when(c == nch - 1)
    def _():
        mo_ref[...] = macc[...] / ml[...]
        do_ref[...] = dacc[...] / dl[...]
        fo_ref[...] = facc[...] / fl[...]


def _decode_call(page_table, layer, caches, mq, dq, fq, mnew, dnew, fnew, lfnew, triu):
    cache_mla_t, cache_diff_t, cache_fox_t, cache_logf_t = caches
    nseq, npages = page_table.shape
    past_len = npages * PAGE
    nch = npages // DEC_PAGES

    def page_spec(rows, j):
        def imap(b, c, pt):
            return (layer, pt[b, (nch - 1 - c) * DEC_PAGES + j], 0, 0)
        return pl.BlockSpec((None, None, rows, PAGE), imap)

    seq3 = lambda r, n: pl.BlockSpec((None, r, n), lambda b, c, pt: (b, 0, 0))
    in_specs = [seq3(8, MLA_ROW), seq3(8, 128), seq3(8, 128), seq3(1, MLA_ROW), seq3(1, 256), seq3(1, 256),
                seq3(8, 1), pl.BlockSpec((PAGE, PAGE), lambda b, c, pt: (0, 0))]
    in_specs += [page_spec(MLA_ROW, j) for j in range(DEC_PAGES)]
    in_specs += [page_spec(256, j) for j in range(DEC_PAGES)]
    in_specs += [page_spec(256, j) for j in range(DEC_PAGES)]
    in_specs += [page_spec(8, j) for j in range(DEC_PAGES)]
    small = lambda: [pltpu.VMEM((8, 1), F32), pltpu.VMEM((8, 1), F32), pltpu.VMEM((8, 128), F32)]
    grid_spec = pltpu.PrefetchScalarGridSpec(
        num_scalar_prefetch=1,
        grid=(nseq, nch),
        in_specs=in_specs,
        out_specs=[seq3(8, 128), seq3(8, 128), seq3(8, 128)],
        scratch_shapes=small() + small() + small() + [pltpu.VMEM((8, 1), F32)],
    )
    return pl.pallas_call(
        functools.partial(_decode_kernel, past_len),
        grid_spec=grid_spec,
        out_shape=[jax.ShapeDtypeStruct((nseq, 8, 128), F32)] * 3,
        compiler_params=_cparams(("arbitrary", "arbitrary")),
        name="decode_attn",
    )(page_table, mq, dq, fq, mnew, dnew, fnew, lfnew, triu,
      *([cache_mla_t] * DEC_PAGES), *([cache_diff_t] * DEC_PAGES), *([cache_fox_t] * DEC_PAGES),
      *([cache_logf_t] * DEC_PAGES))


def _decode_finish_kernel(lam_init, d_ref, f_ref, lam_ref, subln_ref, do_ref, fo_ref):
    lam = _lambda_value(lam_ref, lam_init)
    do_ref[...] = _diff_combine([d_ref[j] for j in range(8)], lam, subln_ref[...], lam_init).astype(BF16)
    fo_ref[...] = _fox_select([f_ref[j] for j in range(4)]).astype(BF16)


def _decode_finish_call(d_blocks, f_blocks, lam_p, subln, lam_init):
    n = d_blocks.shape[1]
    return pl.pallas_call(
        functools.partial(_decode_finish_kernel, lam_init),
        out_shape=[jax.ShapeDtypeStruct((n, 256), BF16)] * 2,
        name="decode_finish",
    )(d_blocks, f_blocks, lam_p, subln)


def _post_kernel(decode, x_ref, olat_ref, diff_ref, fox_ref, gate_a_ref, shift_f_ref, scale_f_ref, gate_f_ref,
                 g_post_ref, g_fpre_ref, g_fpost_ref, wuv_ref, wo_ref, wa_ref, wg_ref, cw_ref, wd_ref, *rest):
    if decode:
        s0_ref, s1_ref, xo_ref, a_out_ref, y_sc = rest
    else:
        xo_ref, tail_ref, a_sc, y_sc = rest
    tm = x_ref.shape[0]
    olat = olat_ref[...]
    mla = [_dot(olat[:, j * 256:(j + 1) * 256], wuv_ref[j]).astype(BF16) for j in range(4)]
    att = jnp.concatenate(mla + [diff_ref[...], fox_ref[...]], axis=1)
    mix = _dot(att, wo_ref[...])
    x1 = x_ref[...] + gate_a_ref[...] * _rms(mix, g_post_ref[...])
    h = (_rms(x1, g_fpre_ref[...]) * (1.0 + scale_f_ref[...]) + shift_f_ref[...]).astype(BF16)

    if not decode:
        @pl.when(pl.program_id(1) == 0)
        def _():
            a_sc[:, 0:8, :] = jnp.zeros((FFN_NCHUNK, 8, FFN_CHUNK), F32)
    y_sc[...] = jnp.zeros(y_sc.shape, F32)

    def chunk(ci, carry):
        a = _dot(h, wa_ref[ci])
        g = _dot(h, wg_ref[ci])
        cw = cw_ref[ci]
        if decode:
            a_out_ref[ci] = a
            conv = cw[3:4] + cw[0:1] * s0_ref[ci] + cw[1:2] * s1_ref[ci] + cw[2:3] * a
        else:
            a_sc[ci, 8:8 + tm, :] = a
            conv = (cw[3:4] + cw[0:1] * a_sc[ci, 6:6 + tm, :] + cw[1:2] * a_sc[ci, 7:7 + tm, :] + cw[2:3] * a)
            last = a_sc[ci, tm:tm + 8, :]
            a_sc[ci, 0:8, :] = last
            tail_ref[ci] = last
        gelu = 0.5 * conv * (1.0 + jnp.tanh(math.sqrt(2.0 / math.pi) * (conv + 0.044715 * (conv * conv * conv))))
        y_sc[...] += _dot((gelu * g).astype(BF16), wd_ref[ci])
        return carry

    lax.fori_loop(0, FFN_NCHUNK, chunk, 0)
    xo_ref[...] = x1 + gate_f_ref[...] * _rms(y_sc[...], g_fpost_ref[...])


def _post_call(x, olat, diff_o, fox_o, mod, lw, tm, state=None):
    B, T, _ = x.shape
    R = mod.shape[1]
    rb = 1 if R == 1 else tm
    decode = state is not None
    mod_map = (lambda c: (lambda b, t: (b, 0, c))) if R == 1 else (lambda c: (lambda b, t: (b, t, c)))
    act = lambda n: pl.BlockSpec((None, tm, n), lambda b, t: (b, t, 0))
    w2 = lambda shape: _const_spec(shape, lambda b, t: (0, 0))
    w3 = lambda shape: _const_spec(shape, lambda b, t: (0, 0, 0))
    in_specs = [act(D_MODEL), act(MLA_HEADS * 128), act(256), act(256),
                pl.BlockSpec((None, rb, D_MODEL), mod_map(2)), pl.BlockSpec((None, rb, D_MODEL), mod_map(3)),
                pl.BlockSpec((None, rb, D_MODEL), mod_map(4)), pl.BlockSpec((None, rb, D_MODEL), mod_map(5)),
                w2((1, D_MODEL)), w2((1, D_MODEL)), w2((1, D_MODEL)),
                w3((4, 256, 128)), w2((D_MODEL, D_MODEL)),
                w3((FFN_NCHUNK, D_MODEL, FFN_CHUNK)), w3((FFN_NCHUNK, D_MODEL, FFN_CHUNK)),
                w3((FFN_NCHUNK, 8, FFN_CHUNK)), w3((FFN_NCHUNK, FFN_CHUNK, D_MODEL))]
    args = [x, olat, diff_o, fox_o, mod, mod, mod, mod, lw["g_attn_post"], lw["g_ffn_pre"], lw["g_ffn_post"],
            lw["wuv_bd"], lw["w_o"], lw["w_a"], lw["w_g"], lw["conv"], lw["w_down"]]
    if decode:
        chunked = pl.BlockSpec((FFN_NCHUNK, tm, FFN_CHUNK), lambda b, t: (0, t, 0))
        in_specs += [chunked] * 2
        args += [s.reshape(T, FFN_NCHUNK, FFN_CHUNK).transpose(1, 0, 2) for s in state]
        out_specs = [act(D_MODEL), chunked]
        out_shape = [jax.ShapeDtypeStruct((B, T, D_MODEL), F32),
                     jax.ShapeDtypeStruct((FFN_NCHUNK, T, FFN_CHUNK), F32)]
        scratch = [pltpu.VMEM((tm, D_MODEL), F32)]
    else:
        out_specs = [act(D_MODEL), pl.BlockSpec((None, FFN_NCHUNK, 8, FFN_CHUNK), lambda b, t: (b, 0, 0, 0))]
        out_shape = [jax.ShapeDtypeStruct((B, T, D_MODEL), F32),
                     jax.ShapeDtypeStruct((B, FFN_NCHUNK, 8, FFN_CHUNK), F32)]
        scratch = [pltpu.VMEM((FFN_NCHUNK, tm + 8, FFN_CHUNK), F32), pltpu.VMEM((tm, D_MODEL), F32)]
    return pl.pallas_call(
        functools.partial(_post_kernel, decode),
        grid=(B, T // tm),
        in_specs=in_specs,
        out_specs=out_specs,
        out_shape=out_shape,
        scratch_shapes=scratch,
        compiler_params=_cparams(("arbitrary", "arbitrary")),
        name="post_decode" if decode else "post",
    )(*args)


def _prep_weights(w_in, mla_w_uq, mla_w_uk, mla_w_uv, w_o, ffn_w_up, ffn_conv_w, ffn_conv_b, ffn_w_down):
    L = N_LAYERS
    zeros = lambda n: jnp.zeros((L, D_MODEL, n), F32)
    o = 0
    seg = {}
    for name, n in (("q_lat", 256), ("kv_lat", 128), ("k_r", 32), ("d_q", 256), ("d_k", 128), ("d_v", 128),
                    ("f_q", 256), ("f_k", 128), ("f_v", 128), ("f_z", 4)):
        seg[name] = w_in[:, :, o:o + n]
        o += n
    k_r = seg["k_r"]
    k_r_sw = jnp.concatenate([k_r[..., 16:], k_r[..., :16]], axis=-1)
    cols = [seg["q_lat"], seg["kv_lat"], k_r, zeros(96), k_r_sw, zeros(96), seg["f_z"], zeros(124)]
    for g in range(2):
        for r in range(2):
            for i in range(2):
                src = seg["d_q"][..., g * 128 + r * 64 + i * 32:g * 128 + r * 64 + (i + 1) * 32]
                lo = g * 64 + i * 32
                cols += [zeros(lo), src, zeros(128 - lo - 32)]
    cols += [seg["d_k"], seg["d_v"]]
    for g in range(2):
        for r in range(2):
            src = seg["f_q"][..., g * 128 + r * 64:g * 128 + (r + 1) * 64]
            cols += [zeros(g * 64), src, zeros(64 - g * 64)]
    cols += [seg["f_k"], seg["f_v"]]
    win = jnp.concatenate([c for c in cols if c.shape[-1] > 0], axis=-1).astype(BF16)

    uq = mla_w_uq.reshape(L, MLA_Q_LORA, MLA_HEADS, MLA_NOPE + MLA_ROPE)
    wq_nope = uq[..., :MLA_NOPE].reshape(L, MLA_Q_LORA, MLA_HEADS * MLA_NOPE).astype(BF16)
    pe = uq[..., MLA_NOPE:]
    wq_pe = pe.reshape(L, MLA_Q_LORA, MLA_HEADS * MLA_ROPE).astype(BF16)
    wq_pe_sw = jnp.concatenate([pe[..., 16:], pe[..., :16]], axis=-1).reshape(
        L, MLA_Q_LORA, MLA_HEADS * MLA_ROPE).astype(BF16)
    eye = jnp.eye(MLA_HEADS, dtype=F32)
    wuk_bd = jnp.einsum("lchn,hk->lhnkc", mla_w_uk, eye).reshape(L, MLA_HEADS * MLA_NOPE,
                                                               MLA_HEADS * MLA_KV_LORA).astype(BF16)
    eye2 = jnp.eye(2, dtype=F32)
    uv = mla_w_uv.reshape(L, MLA_KV_LORA, 4, 2, MLA_V)
    wuv_bd = jnp.einsum("lcjev,ef->ljecfv", uv, eye2).reshape(L, 4, 256, 128).astype(BF16)

    w_a = ffn_w_up[:, :, :D_FF].reshape(L, D_MODEL, FFN_NCHUNK, FFN_CHUNK).transpose(0, 2, 1, 3).astype(BF16)
    w_g = ffn_w_up[:, :, D_FF:].reshape(L, D_MODEL, FFN_NCHUNK, FFN_CHUNK).transpose(0, 2, 1, 3).astype(BF16)
    w_down = ffn_w_down.reshape(L, FFN_NCHUNK, FFN_CHUNK, D_MODEL).astype(BF16)
    conv = jnp.concatenate([ffn_conv_w, ffn_conv_b[:, None, :], jnp.zeros((L, 4, D_FF), F32)], axis=1)
    conv = conv.reshape(L, 8, FFN_NCHUNK, FFN_CHUNK).transpose(0, 2, 1, 3)
    return dict(win=win, wq_nope=wq_nope, wq_pe=wq_pe, wq_pe_sw=wq_pe_sw, wuk_bd=wuk_bd, wuv_bd=wuv_bd,
                w_o=w_o.astype(BF16), w_a=w_a, w_g=w_g, w_down=w_down, conv=conv)


def _rope_tables(pos):
    half = MLA_ROPE // 2
    freqs = ROPE_THETA ** (-jnp.arange(half, dtype=F32) / half)
    ang = pos.astype(F32)[:, None] * freqs
    cos, sin = jnp.cos(ang), jnp.sin(ang)
    ck = jnp.concatenate([cos, cos], axis=-1)
    sk = jnp.concatenate([-sin, sin], axis=-1)
    return jnp.tile(ck, (1, MLA_HEADS)), jnp.tile(sk, (1, MLA_HEADS)), ck, sk


def kernel(x_prompt, x_sample, c_prompt, c_sample, cache_mla, cache_diff_kv, cache_fox_kv, cache_fox_logf, state_conv, page_table, w_ada, b_ada, g_attn_pre, g_attn_post, g_ffn_pre, g_ffn_post, w_in, mla_q_norm, mla_kv_norm, mla_w_uq, mla_w_uk, mla_w_uv, diff_lambda, diff_subln, fox_b_f, w_o, ffn_w_up, ffn_conv_w, ffn_conv_b, ffn_w_down):
    L = N_LAYERS
    B, T, _ = x_prompt.shape
    DB = x_sample.shape[0]
    n_pool = cache_mla.shape[1]
    past_len = page_table.shape[1] * PAGE

    wts = _prep_weights(w_in, mla_w_uq, mla_w_uk, mla_w_uv, w_o, ffn_w_up, ffn_conv_w, ffn_conv_b, ffn_w_down)
    tabs_p = _rope_tables(jnp.arange(T, dtype=jnp.int32))
    tabs_s = _rope_tables(jnp.full((DB,), past_len, jnp.int32))
    tri = (jnp.arange(T)[:, None] <= jnp.arange(T)[None, :]).astype(BF16)
    triu = (jnp.arange(PAGE)[:, None] > jnp.arange(PAGE)[None, :]).astype(BF16)

    cache_mla_t = jnp.transpose(cache_mla, (0, 1, 3, 2))
    cache_diff_t = jnp.transpose(cache_diff_kv, (0, 1, 3, 4, 5, 2)).reshape(L, n_pool, 256, PAGE)
    cache_fox_t = jnp.transpose(cache_fox_kv, (0, 1, 3, 4, 5, 2)).reshape(L, n_pool, 256, PAGE)
    cache_logf_t = jnp.pad(jnp.transpose(cache_fox_logf, (0, 1, 3, 2)), ((0, 0), (0, 0), (0, 4), (0, 0)))
    caches = (cache_mla_t, cache_diff_t, cache_fox_t, cache_logf_t)

    mod_all = _ada_call(jnp.concatenate([c_prompt, c_sample], axis=0), w_ada, b_ada)

    xp = x_prompt
    xs = x_sample.reshape(1, DB, D_MODEL)
    outs = [[] for _ in range(10)]
    for l in range(L):
        lam_init = 0.8 - 0.6 * math.exp(-0.3 * l)
        lw = {k: v[l] for k, v in wts.items()}
        lw.update(g_attn_pre=g_attn_pre[l][None], g_attn_post=g_attn_post[l][None], g_ffn_pre=g_ffn_pre[l][None],
                  g_ffn_post=g_ffn_post[l][None], mla_q_norm=mla_q_norm[l][None], mla_kv_norm=mla_kv_norm[l][None],
                  fox_b_f=fox_b_f[l][None])
        lam_p = diff_lambda[l]
        subln = diff_subln[l].reshape(1, DIFF_HEADS * HEAD_DIM)
        mod_p = mod_all[l, :B].reshape(B, 1, 6 * D_MODEL)
        mod_s = mod_all[l, B:].reshape(1, DB, 6 * D_MODEL)

        mq, mrow, dq, drow, fq, frow, logf = _pre_call(xp, mod_p, lw, tabs_p, PRE_TM)
        logf_t = jnp.pad(jnp.transpose(logf, (0, 2, 1)), ((0, 0), (0, 4), (0, 0)))
        f_cum = _cumsum_call(logf_t, tri)
        f_cum_t = jnp.transpose(f_cum[:, :FOX_HEADS], (0, 2, 1))
        olat = _mla_attn_call(mq, mrow)
        diff_o = _diff_attn_call(dq, drow, lam_p, subln, lam_init)
        f_cum_k = f_cum.reshape(B, 8, T // ATT_TQ, ATT_TQ).transpose(0, 2, 1, 3)
        fox_o = _fox_attn_call(fq, frow, f_cum_k, f_cum_t)
        xp, tail = _post_call(xp, olat, diff_o, fox_o, mod_p, lw, POST_TM)
        conv_p = tail[:, :, 6:8, :].transpose(0, 2, 1, 3).reshape(B, 2, D_FF)

        mq_s, mrow_s, dq_s, drow_s, fq_s, frow_s, logf_s = _pre_call(xs, mod_s, lw, tabs_s, DB)
        fq_s8 = jnp.pad(jnp.transpose(fq_s[0], (1, 0, 2)), ((0, 0), (0, 4), (0, 0)))
        lf_s8 = jnp.pad(logf_s[0], ((0, 0), (0, 4)))[:, :, None]
        m_o, d_o, f_o = _decode_call(
            page_table, l, caches, jnp.transpose(mq_s[0], (1, 0, 2)), jnp.transpose(dq_s[0], (1, 0, 2)), fq_s8,
            mrow_s.reshape(DB, 1, MLA_ROW), drow_s.reshape(DB, 1, 256), frow_s.reshape(DB, 1, 256), lf_s8, triu)
        diff_s, fox_s = _decode_finish_call(jnp.transpose(d_o, (1, 0, 2)), jnp.transpose(f_o[:, :4], (1, 0, 2)),
                                            lam_p, subln, lam_init)
        olat_s = m_o.reshape(1, DB, MLA_HEADS * 128).astype(BF16)
        st = state_conv[l]
        xs, a_s = _post_call(xs, olat_s, diff_s[None], fox_s[None], mod_s, lw, DB, state=(st[:, 0], st[:, 1]))
        conv_s = jnp.stack([st[:, 1], a_s.transpose(1, 0, 2).reshape(DB, D_FF)], axis=1)

        for i, v in enumerate((mrow, mrow_s.reshape(DB, 1, MLA_ROW), drow.reshape(B, T, 2, 2, HEAD_DIM),
                               drow_s.reshape(DB, 1, 2, 2, HEAD_DIM), frow.reshape(B, T, 2, 2, HEAD_DIM),
                               frow_s.reshape(DB, 1, 2, 2, HEAD_DIM), logf, logf_s.reshape(DB, 1, FOX_HEADS),
                               conv_p, conv_s)):
            outs[i].append(v)

    return (xp, xs.reshape(DB, 1, D_MODEL)) + tuple(jnp.stack(o) for o in outs)
```

```python
import functools
import math

import jax
import jax.numpy as jnp
import numpy as np
from jax import lax
from jax.experimental import pallas as pl
from jax.experimental.pallas import tpu as pltpu

F32 = jnp.float32
BF16 = jnp.bfloat16

D_MODEL = 1024
D_FF = 2816
N_LAYERS = 4
MLA_HEADS = 8
MLA_Q_LORA = 256
MLA_KV_LORA = 128
MLA_NOPE = 64
MLA_ROPE = 32
MLA_V = 64
MLA_ROW = MLA_KV_LORA + MLA_ROPE
DIFF_HEADS = 4
DIFF_HALF = 32
FOX_HEADS = 4
HEAD_DIM = 64
ROPE_THETA = 10000.0
RMS_EPS = 1e-6
NEG_INF = -1e30
PAGE = 128

C_QLAT = 0
C_KVLAT = 256
C_KR = 384
C_KRS = 512
C_FZ = 640
C_DQ = 768
C_DROW = 1792
C_FQ = 2048
C_FROW = 2560
C_TOTAL = 2816

FFN_CHUNK = 256
FFN_NCHUNK = D_FF // FFN_CHUNK

PRE_TM = 512
POST_TM = 512
ATT_TQ = 128
ATT_TK = 512
DEC_PAGES = 32
LOG2E = math.log2(math.e)
VMEM_LIMIT = 56 * 1024 * 1024


def _cparams(sem):
    return pltpu.CompilerParams(dimension_semantics=sem, vmem_limit_bytes=VMEM_LIMIT)


def _const_spec(shape, index_map):
    return pl.BlockSpec(shape, index_map, pipeline_mode=pl.Buffered(1))


def _rms(x, g):
    return x * lax.rsqrt(jnp.mean(x * x, axis=-1, keepdims=True) + RMS_EPS) * g


def _dot(a, b):
    return jnp.dot(a, b, preferred_element_type=F32)


def _dot_nt(a, b):
    return lax.dot_general(a, b, (((1,), (1,)), ((), ())), preferred_element_type=F32)


def _ada_kernel(c_ref, w_ref, b_ref, o_ref):
    c = c_ref[...]
    s = (c * jax.nn.sigmoid(c)).astype(BF16)
    o_ref[...] = _dot(s, w_ref[...].astype(BF16)) + b_ref[...]


def _ada_call(c_all, w_ada, b_ada):
    n = c_all.shape[0]
    tn = 1536
    return pl.pallas_call(
        _ada_kernel,
        grid=(N_LAYERS, 6 * D_MODEL // tn),
        in_specs=[pl.BlockSpec((n, D_MODEL), lambda l, j: (0, 0)),
                  pl.BlockSpec((None, D_MODEL, tn), lambda l, j: (l, 0, j)),
                  pl.BlockSpec((None, 1, tn), lambda l, j: (l, 0, j))],
        out_specs=pl.BlockSpec((None, n, tn), lambda l, j: (l, 0, j)),
        out_shape=jax.ShapeDtypeStruct((N_LAYERS, n, 6 * D_MODEL), F32),
        compiler_params=_cparams(("arbitrary", "arbitrary")),
        name="ada",
    )(c_all, w_ada, b_ada.reshape(N_LAYERS, 1, 6 * D_MODEL))


def _pre_kernel(x_ref, shift_ref, scale_ref, g_ref, win_ref, qn_ref, kvn_ref, wqn_ref, wqp_ref, wqps_ref,
                wuk_ref, bf_ref, cq_ref, sq_ref, ck_ref, sk_ref,
                mq_ref, mrow_ref, dq_ref, drow_ref, fq_ref, frow_ref, logf_ref):
    x = x_ref[...]
    h = _rms(x, g_ref[...]) * (1.0 + scale_ref[...]) + shift_ref[...]
    z = _dot(h.astype(BF16), win_ref[...])

    qn = _rms(z[:, C_QLAT:C_QLAT + MLA_Q_LORA], qn_ref[...]).astype(BF16)
    q_nope = _dot(qn, wqn_ref[...])
    q_pe = _dot(qn, wqp_ref[...]) * cq_ref[...] + _dot(qn, wqps_ref[...]) * sq_ref[...]
    q_abs = _dot(q_nope.astype(BF16), wuk_ref[...])
    for hh in range(MLA_HEADS):
        mq_ref[hh, :, 0:MLA_KV_LORA] = q_abs[:, hh * 128:(hh + 1) * 128].astype(BF16)
        mq_ref[hh, :, MLA_KV_LORA:MLA_ROW] = q_pe[:, hh * MLA_ROPE:(hh + 1) * MLA_ROPE].astype(BF16)

    mrow_ref[:, 0:MLA_KV_LORA] = _rms(z[:, C_KVLAT:C_KVLAT + MLA_KV_LORA], kvn_ref[...])
    mrow_ref[:, MLA_KV_LORA:MLA_ROW] = (z[:, C_KR:C_KR + MLA_ROPE] * ck_ref[...]
                                        + z[:, C_KRS:C_KRS + MLA_ROPE] * sk_ref[...])

    for j in range(8):
        dq_ref[j] = z[:, C_DQ + j * 128:C_DQ + (j + 1) * 128].astype(BF16)
    drow_ref[...] = z[:, C_DROW:C_DROW + 256]
    for j in range(4):
        fq_ref[j] = z[:, C_FQ + j * 128:C_FQ + (j + 1) * 128].astype(BF16)
    frow_ref[...] = z[:, C_FROW:C_FROW + 256]

    fz = z[:, C_FZ:C_FZ + FOX_HEADS] + bf_ref[...]
    logf_ref[...] = jnp.minimum(fz, 0.0) - jnp.log1p(jnp.exp(-jnp.abs(fz)))


def _pre_call(x, mod, lw, tabs, tm):
    B, T, _ = x.shape
    R = mod.shape[1]
    rb = 1 if R == 1 else tm
    mod_map = (lambda c: (lambda b, t: (b, 0, c))) if R == 1 else (lambda c: (lambda b, t: (b, t, c)))
    cq, sq, ck, sk = tabs
    w2 = lambda shape: _const_spec(shape, lambda b, t: (0, 0))
    row = lambda n: pl.BlockSpec((tm, n), lambda b, t: (t, 0))
    outs = pl.pallas_call(
        _pre_kernel,
        grid=(B, T // tm),
        in_specs=[pl.BlockSpec((None, tm, D_MODEL), lambda b, t: (b, t, 0)),
                  pl.BlockSpec((None, rb, D_MODEL), mod_map(0)),
                  pl.BlockSpec((None, rb, D_MODEL), mod_map(1)),
                  w2((1, D_MODEL)), w2((D_MODEL, C_TOTAL)), w2((1, MLA_Q_LORA)), w2((1, MLA_KV_LORA)),
                  w2((MLA_Q_LORA, 512)), w2((MLA_Q_LORA, 256)), w2((MLA_Q_LORA, 256)), w2((512, 1024)),
                  w2((1, FOX_HEADS)), row(256), row(256), row(MLA_ROPE), row(MLA_ROPE)],
        out_specs=[pl.BlockSpec((None, 8, tm, MLA_ROW), lambda b, t: (b, 0, t, 0)),
                   pl.BlockSpec((None, tm, MLA_ROW), lambda b, t: (b, t, 0)),
                   pl.BlockSpec((None, 8, tm, 128), lambda b, t: (b, 0, t, 0)),
                   pl.BlockSpec((None, tm, 256), lambda b, t: (b, t, 0)),
                   pl.BlockSpec((None, 4, tm, 128), lambda b, t: (b, 0, t, 0)),
                   pl.BlockSpec((None, tm, 256), lambda b, t: (b, t, 0)),
                   pl.BlockSpec((None, tm, FOX_HEADS), lambda b, t: (b, t, 0))],
        out_shape=[jax.ShapeDtypeStruct((B, 8, T, MLA_ROW), BF16),
                   jax.ShapeDtypeStruct((B, T, MLA_ROW), F32),
                   jax.ShapeDtypeStruct((B, 8, T, 128), BF16),
                   jax.ShapeDtypeStruct((B, T, 256), F32),
                   jax.ShapeDtypeStruct((B, 4, T, 128), BF16),
                   jax.ShapeDtypeStruct((B, T, 256), F32),
                   jax.ShapeDtypeStruct((B, T, FOX_HEADS), F32)],
        compiler_params=_cparams(("arbitrary", "arbitrary")),
        name="pre",
    )(x, mod, mod, lw["g_attn_pre"], lw["win"], lw["mla_q_norm"], lw["mla_kv_norm"], lw["wq_nope"],
      lw["wq_pe"], lw["wq_pe_sw"], lw["wuk_bd"], lw["fox_b_f"], cq, sq, ck, sk)
    return outs


def _diag_mask(nblk, tq):
    rows = lax.broadcasted_iota(jnp.int32, (nblk * tq, tq), 0) & (tq - 1)
    cols = lax.broadcasted_iota(jnp.int32, (nblk * tq, tq), 1)
    return cols <= rows


def _alibi_slope(head):
    return jnp.where(head == 0, 2.0 ** -2, jnp.where(head == 1, 2.0 ** -4,
                                                     jnp.where(head == 2, 2.0 ** -6, 2.0 ** -8))).astype(F32)


def _lambda_value(lam_ref, lam_init):
    lv = lam_ref[...]
    a = jnp.sum(lv[0:1, :] * lv[1:2, :], axis=-1, keepdims=True)
    b = jnp.sum(lv[2:3, :] * lv[3:4, :], axis=-1, keepdims=True)
    return jnp.exp(a) - jnp.exp(b) + lam_init


def _diff_combine(blocks, lam, subln, lam_init):
    outs = []
    for g in range(2):
        for r in range(2):
            o = blocks[g * 4 + r * 2] - lam * blocks[g * 4 + r * 2 + 1]
            o = o[:, g * HEAD_DIM:(g + 1) * HEAD_DIM]
            hh = g * 2 + r
            outs.append(_rms(o, subln[:, hh * HEAD_DIM:(hh + 1) * HEAD_DIM]) * (1.0 - lam_init))
    return jnp.concatenate(outs, axis=1)


def _fox_select(blocks):
    return jnp.concatenate([blocks[g * 2 + r][:, g * HEAD_DIM:(g + 1) * HEAD_DIM]
                            for g in range(2) for r in range(2)], axis=1)


def _softmax_step(chunks, v_bf, m_ref, l_ref, acc_ref):
    mx = chunks[0]
    for c in chunks[1:]:
        mx = jnp.maximum(mx, c)
    m_prev = m_ref[...]
    m_new = jnp.maximum(m_prev, jnp.max(mx, axis=-1, keepdims=True))
    alpha = jnp.exp2(m_prev - m_new)
    ps = [jnp.exp2(c - m_new) for c in chunks]
    lsum = ps[0]
    for p in ps[1:]:
        lsum = lsum + p
    l_ref[...] = alpha * l_ref[...] + lsum
    p_bf = jnp.concatenate([p.astype(BF16) for p in ps], axis=1) if len(ps) > 1 else ps[0].astype(BF16)
    acc_ref[...] = alpha * acc_ref[...] + _dot(p_bf, v_bf)
    m_ref[...] = m_new


def _attn_kernel(lam_init, mq_ref, dq_ref, fq_ref, mkv_ref, dkv_ref, fkv_ref, fk_ref, fqc_ref, lam_ref, subln_ref,
                 mo_ref, do_ref, fo_ref, mm, ml, macc, dm, dl, dacc, fm, fl, facc, sb_sc, sl_sc):
    tq, tk = ATT_TQ, ATT_TK
    per_tile = tk // tq
    qi = pl.program_id(1)
    m_c = LOG2E / math.sqrt(MLA_NOPE + MLA_ROPE)
    d_c = LOG2E / math.sqrt(DIFF_HALF)
    f_c = LOG2E / math.sqrt(HEAD_DIM)

    @pl.when((pl.program_id(0) == 0) & (qi == 0))
    def _():
        rows = lax.broadcasted_iota(jnp.int32, (8 * tq, tk), 0)
        cols = lax.broadcasted_iota(jnp.int32, (8 * tq, tk), 1)
        slope = _alibi_slope(rows >> (int(math.log2(tq)) + 1)) * LOG2E
        sb_sc[...] = slope * ((rows & (tq - 1)) - cols).astype(F32)
        sl_sc[...] = slope[:, 0:128]

    for m_ref, l_ref, acc_ref in ((mm, ml, macc), (dm, dl, dacc), (fm, fl, facc)):
        m_ref[...] = jnp.full(m_ref.shape, NEG_INF, F32)
        l_ref[...] = jnp.zeros(l_ref.shape, F32)
        acc_ref[...] = jnp.zeros(acc_ref.shape, F32)

    mq = mq_ref[...].reshape(MLA_HEADS * tq, MLA_ROW)
    dq = dq_ref[...].reshape(8 * tq, 128)
    fq = fq_ref[...].reshape(4 * tq, 128)
    fqc = fqc_ref[...] * LOG2E
    fq_bias = jnp.concatenate([jnp.broadcast_to(fqc[:, hh:hh + 1], (tq, 128)) for hh in range(FOX_HEADS)], axis=0)

    def tile(kt, width, diag, off):
        nc = width // 128
        k0 = pl.multiple_of(kt * tk, tk)
        sl = lambda s, j: s[:, j * 128:(j + 1) * 128]

        kb = mkv_ref[pl.ds(k0, width), :].astype(BF16)
        s = _dot_nt(mq, kb) * m_c
        chunks = [sl(s, j) for j in range(nc)]
        if diag:
            chunks[-1] = jnp.where(_diag_mask(MLA_HEADS, tq), chunks[-1], NEG_INF)
        _softmax_step(chunks, kb[:, 0:MLA_KV_LORA], mm, ml, macc)

        kb = dkv_ref[pl.ds(k0, width), 0:128].astype(BF16)
        vb = dkv_ref[pl.ds(k0, width), 128:256].astype(BF16)
        s = _dot_nt(dq, kb) * d_c
        slope_off = sl_sc[...] * off
        chunks = [(sl(s, j) - sb_sc[:, j * 128:(j + 1) * 128]) - slope_off for j in range(nc)]
        if diag:
            chunks[-1] = jnp.where(_diag_mask(8, tq), chunks[-1], NEG_INF)
        _softmax_step(chunks, vb, dm, dl, dacc)

        kb = fkv_ref[pl.ds(k0, width), 0:128].astype(BF16)
        vb = fkv_ref[pl.ds(k0, width), 128:256].astype(BF16)
        s = _dot_nt(fq, kb) * f_c
        fk = fk_ref[kt] * LOG2E
        chunks = []
        for j in range(nc):
            fk_j = jnp.concatenate([jnp.broadcast_to(fk[hh:hh + 1, j * 128:(j + 1) * 128], (tq, 128))
                                    for hh in range(FOX_HEADS)], axis=0)
            chunks.append((sl(s, j) + fq_bias) - fk_j)
        if diag:
            chunks[-1] = jnp.where(_diag_mask(4, tq), chunks[-1], NEG_INF)
        _softmax_step(chunks, vb, fm, fl, facc)

    n_full = qi // per_tile
    rem = qi - n_full * per_tile

    def body(kt, c):
        tile(kt, tk, False, (qi * tq - kt * tk).astype(F32))
        return c

    lax.fori_loop(0, n_full, body, 0)
    for r in range(per_tile):
        @pl.when(rem == r)
        def _():
            tile(n_full, (r + 1) * tq, True, float(r * tq))

    def finish(l_ref, acc_ref):
        return acc_ref[...] / jnp.sum(l_ref[...], axis=-1, keepdims=True)

    o = finish(ml, macc)
    for hh in range(MLA_HEADS):
        mo_ref[:, hh * 128:(hh + 1) * 128] = o[hh * tq:(hh + 1) * tq].astype(BF16)
    o = finish(dl, dacc)
    lam = _lambda_value(lam_ref, lam_init)
    do_ref[...] = _diff_combine([o[j * tq:(j + 1) * tq] for j in range(8)], lam, subln_ref[...],
                                lam_init).astype(BF16)
    o = finish(fl, facc)
    fo_ref[...] = _fox_select([o[j * tq:(j + 1) * tq] for j in range(4)]).astype(BF16)


def _attn_call(mq, dq, fq, mrow, drow, frow, f_cum_k, f_cum_t, lam_p, subln, lam_init):
    B, _, T, _ = mq.shape
    tq, tk = ATT_TQ, ATT_TK
    qblk = lambda n, d: pl.BlockSpec((None, n, tq, d), lambda b, i: (b, 0, i, 0))
    full = lambda d: pl.BlockSpec((None, T, d), lambda b, i: (b, 0, 0))
    oblk = lambda d: pl.BlockSpec((None, tq, d), lambda b, i: (b, i, 0))
    stat = lambda n: [pltpu.VMEM((n * tq, 128), F32)] * 3
    return pl.pallas_call(
        functools.partial(_attn_kernel, lam_init),
        grid=(B, T // tq),
        in_specs=[qblk(MLA_HEADS, MLA_ROW), qblk(8, 128), qblk(4, 128), full(MLA_ROW), full(256), full(256),
                  pl.BlockSpec((None, T // tk, 8, tk), lambda b, i: (b, 0, 0, 0)),
                  pl.BlockSpec((None, tq, FOX_HEADS), lambda b, i: (b, i, 0)),
                  pl.BlockSpec((4, DIFF_HALF), lambda b, i: (0, 0)),
                  pl.BlockSpec((1, 256), lambda b, i: (0, 0))],
        out_specs=[oblk(MLA_HEADS * 128), oblk(256), oblk(256)],
        out_shape=[jax.ShapeDtypeStruct((B, T, MLA_HEADS * 128), BF16),
                   jax.ShapeDtypeStruct((B, T, 256), BF16), jax.ShapeDtypeStruct((B, T, 256), BF16)],
        scratch_shapes=stat(MLA_HEADS) + stat(8) + stat(4) + [pltpu.VMEM((8 * tq, tk), F32),
                                                              pltpu.VMEM((8 * tq, 128), F32)],
        compiler_params=_cparams(("arbitrary", "arbitrary")),
        name="prefill_attn",
    )(mq, dq, fq, mrow, drow, frow, f_cum_k, f_cum_t, lam_p, subln)


def _split3(x):
    hi = x.astype(BF16)
    r = x - hi.astype(F32)
    mid = r.astype(BF16)
    lo = (r - mid.astype(F32)).astype(BF16)
    return hi, mid, lo


def _cumsum_kernel(x_ref, tri_ref, o_ref):
    hi, mid, lo = _split3(x_ref[...])
    tri = tri_ref[...]
    o_ref[...] = (_dot(hi, tri) + _dot(mid, tri)) + _dot(lo, tri)


def _cumsum_call(x, tri):
    B, R, T = x.shape
    return pl.pallas_call(
        _cumsum_kernel,
        grid=(B,),
        in_specs=[pl.BlockSpec((None, R, T), lambda b: (b, 0, 0)),
                  _const_spec((T, T), lambda b: (0, 0))],
        out_specs=pl.BlockSpec((None, R, T), lambda b: (b, 0, 0)),
        out_shape=jax.ShapeDtypeStruct((B, R, T), F32),
        compiler_params=_cparams(("arbitrary",)),
        name="logf_cumsum",
    )(x, tri)


def _decode_kernel(past_len, pt_ref, mq_ref, dq_ref, fq_ref, mnew_ref, dnew_ref, fnew_ref, lfnew_ref, triu_ref,
                   *rest):
    np_ = DEC_PAGES
    mla_pages = rest[0:np_]
    diff_pages = rest[np_:2 * np_]
    fox_pages = rest[2 * np_:3 * np_]
    logf_pages = rest[3 * np_:4 * np_]
    mo_ref, do_ref, fo_ref = rest[4 * np_:4 * np_ + 3]
    (mm, ml, macc, dm, dl, dacc, fm, fl, facc, carry) = rest[4 * np_ + 3:]
    c = pl.program_id(1)
    nch = pl.num_programs(1)
    width = np_ * PAGE

    mq = mq_ref[...]
    dq = dq_ref[...]
    fq = fq_ref[...]
    m_scale = 1.0 / math.sqrt(MLA_NOPE + MLA_ROPE)
    d_scale = 1.0 / math.sqrt(DIFF_HALF)
    f_scale = 1.0 / math.sqrt(HEAD_DIM)

    @pl.when(c == 0)
    def _():
        def self_init(q, k_new, v_new, scale, m_sc, l_sc, acc_sc):
            kb = k_new.astype(BF16).astype(F32)
            m_sc[...] = jnp.sum(q.astype(F32) * kb, axis=-1, keepdims=True) * scale
            l_sc[...] = jnp.ones(l_sc.shape, F32)
            acc_sc[...] = jnp.broadcast_to(v_new.astype(BF16).astype(F32), acc_sc.shape)

        mnew = mnew_ref[...]
        self_init(mq, mnew, mnew[:, 0:MLA_KV_LORA], m_scale, mm, ml, macc)
        dnew = dnew_ref[...]
        self_init(dq, dnew[:, 0:128], dnew[:, 128:256], d_scale, dm, dl, dacc)
        fnew = fnew_ref[...]
        self_init(fq, fnew[:, 0:128], fnew[:, 128:256], f_scale, fm, fl, facc)
        carry[...] = lfnew_ref[...]

    def update(s, vt_bf, m_sc, l_sc, acc_sc):
        m_prev = m_sc[...]
        m_new = jnp.maximum(m_prev, jnp.max(s, axis=-1, keepdims=True))
        alpha = jnp.exp(m_prev - m_new)
        p = jnp.exp(s - m_new)
        l_sc[...] = alpha * l_sc[...] + jnp.sum(p, axis=-1, keepdims=True)
        acc_sc[...] = alpha * acc_sc[...] + _dot_nt(p.astype(BF16), vt_bf)
        m_sc[...] = m_new

    mla_t = jnp.concatenate([r[...].astype(BF16) for r in mla_pages], axis=1)
    update(_dot(mq, mla_t) * m_scale, mla_t[0:MLA_KV_LORA, :], mm, ml, macc)

    first_pos = (nch - 1 - c) * width
    diff_k = jnp.concatenate([r[0:128, :].astype(BF16) for r in diff_pages], axis=1)
    diff_v = jnp.concatenate([r[128:256, :].astype(BF16) for r in diff_pages], axis=1)
    kpos = first_pos + lax.broadcasted_iota(jnp.int32, (8, width), 1)
    slope = _alibi_slope(lax.broadcasted_iota(jnp.int32, (8, width), 0) >> 1)
    s_d = _dot(dq, diff_k) * d_scale - slope * (past_len - kpos).astype(F32)
    update(s_d, diff_v, dm, dl, dacc)

    fox_k = jnp.concatenate([r[0:128, :].astype(BF16) for r in fox_pages], axis=1)
    fox_v = jnp.concatenate([r[128:256, :].astype(BF16) for r in fox_pages], axis=1)
    lf = [r[...] for r in logf_pages]
    hi, mid, lo = _split3(jnp.concatenate(lf, axis=0))
    triu = triu_ref[...]
    within = (_dot(hi, triu) + _dot(mid, triu)) + _dot(lo, triu)
    run = carry[...]
    bias = [None] * np_
    for j in range(np_ - 1, -1, -1):
        bias[j] = within[j * 8:(j + 1) * 8] + run
        run = run + jnp.sum(lf[j], axis=-1, keepdims=True)
    carry[...] = run
    s_f = _dot(fq, fox_k) * f_scale + jnp.concatenate(bias, axis=1)
    update(s_f, fox_v, fm, fl, facc)

    @pl.when(c == nch - 1)
    def _():
        mo_ref[...] = macc[...] / ml[...]
        do_ref[...] = dacc[...] / dl[...]
        fo_ref[...] = facc[...] / fl[...]


def _decode_call(page_table, layer, caches, mq, dq, fq, mnew, dnew, fnew, lfnew, triu):
    cache_mla_t, cache_diff_t, cache_fox_t, cache_logf_t = caches
    nseq, npages = page_table.shape
    past_len = npages * PAGE
    nch = npages // DEC_PAGES

    def page_spec(rows, j):
        def imap(b, c, pt):
            return (layer, pt[b, (nch - 1 - c) * DEC_PAGES + j], 0, 0)
        return pl.BlockSpec((None, None, rows, PAGE), imap)

    seq3 = lambda r, n: pl.BlockSpec((None, r, n), lambda b, c, pt: (b, 0, 0))
    in_specs = [seq3(8, MLA_ROW), seq3(8, 128), seq3(8, 128), seq3(1, MLA_ROW), seq3(1, 256), seq3(1, 256),
                seq3(8, 1), pl.BlockSpec((PAGE, PAGE), lambda b, c, pt: (0, 0))]
    in_specs += [page_spec(MLA_ROW, j) for j in range(DEC_PAGES)]
    in_specs += [page_spec(256, j) for j in range(DEC_PAGES)]
    in_specs += [page_spec(256, j) for j in range(DEC_PAGES)]
    in_specs += [page_spec(8, j) for j in range(DEC_PAGES)]
    small = lambda: [pltpu.VMEM((8, 1), F32), pltpu.VMEM((8, 1), F32), pltpu.VMEM((8, 128), F32)]
    grid_spec = pltpu.PrefetchScalarGridSpec(
        num_scalar_prefetch=1,
        grid=(nseq, nch),
        in_specs=in_specs,
        out_specs=[seq3(8, 128), seq3(8, 128), seq3(8, 128)],
        scratch_shapes=small() + small() + small() + [pltpu.VMEM((8, 1), F32)],
    )
    return pl.pallas_call(
        functools.partial(_decode_kernel, past_len),
        grid_spec=grid_spec,
        out_shape=[jax.ShapeDtypeStruct((nseq, 8, 128), F32)] * 3,
        compiler_params=_cparams(("arbitrary", "arbitrary")),
        name="decode_attn",
    )(page_table, mq, dq, fq, mnew, dnew, fnew, lfnew, triu,
      *([cache_mla_t] * DEC_PAGES), *([cache_diff_t] * DEC_PAGES), *([cache_fox_t] * DEC_PAGES),
      *([cache_logf_t] * DEC_PAGES))


def _decode_finish_kernel(lam_init, d_ref, f_ref, lam_ref, subln_ref, do_ref, fo_ref):
    lam = _lambda_value(lam_ref, lam_init)
    do_ref[...] = _diff_combine([d_ref[j] for j in range(8)], lam, subln_ref[...], lam_init).astype(BF16)
    fo_ref[...] = _fox_select([f_ref[j] for j in range(4)]).astype(BF16)


def _decode_finish_call(d_blocks, f_blocks, lam_p, subln, lam_init):
    n = d_blocks.shape[1]
    return pl.pallas_call(
        functools.partial(_decode_finish_kernel, lam_init),
        out_shape=[jax.ShapeDtypeStruct((n, 256), BF16)] * 2,
        name="decode_finish",
    )(d_blocks, f_blocks, lam_p, subln)


def _post_kernel(decode, x_ref, olat_ref, diff_ref, fox_ref, gate_a_ref, shift_f_ref, scale_f_ref, gate_f_ref,
                 g_post_ref, g_fpre_ref, g_fpost_ref, wuv_ref, wo_ref, wa_ref, wg_ref, cw_ref, wd_ref, *rest):
    if decode:
        s0_ref, s1_ref, xo_ref, a_out_ref, y_sc = rest
    else:
        xo_ref, tail_ref, a_sc, y_sc = rest
    tm = x_ref.shape[0]
    olat = olat_ref[...]
    mla = [_dot(olat[:, j * 256:(j + 1) * 256], wuv_ref[j]).astype(BF16) for j in range(4)]
    att = jnp.concatenate(mla + [diff_ref[...], fox_ref[...]], axis=1)
    mix = _dot(att, wo_ref[...])
    x1 = x_ref[...] + gate_a_ref[...] * _rms(mix, g_post_ref[...])
    h = (_rms(x1, g_fpre_ref[...]) * (1.0 + scale_f_ref[...]) + shift_f_ref[...]).astype(BF16)

    if not decode:
        @pl.when(pl.program_id(1) == 0)
        def _():
            a_sc[:, 0:8, :] = jnp.zeros((FFN_NCHUNK, 8, FFN_CHUNK), F32)
    y_sc[...] = jnp.zeros(y_sc.shape, F32)

    def chunk(ci, carry):
        a = _dot(h, wa_ref[ci])
        g = _dot(h, wg_ref[ci])
        cw = cw_ref[ci]
        if decode:
            a_out_ref[ci] = a
            conv = cw[3:4] + cw[0:1] * s0_ref[ci] + cw[1:2] * s1_ref[ci] + cw[2:3] * a
        else:
            a_sc[ci, 8:8 + tm, :] = a
            conv = (cw[3:4] + cw[0:1] * a_sc[ci, 6:6 + tm, :] + cw[1:2] * a_sc[ci, 7:7 + tm, :] + cw[2:3] * a)
            last = a_sc[ci, tm:tm + 8, :]
            a_sc[ci, 0:8, :] = last
            tail_ref[ci] = last
        gelu = 0.5 * conv * (1.0 + jnp.tanh(math.sqrt(2.0 / math.pi) * (conv + 0.044715 * (conv * conv * conv))))
        y_sc[...] += _dot((gelu * g).astype(BF16), wd_ref[ci])
        return carry

    lax.fori_loop(0, FFN_NCHUNK, chunk, 0)
    xo_ref[...] = x1 + gate_f_ref[...] * _rms(y_sc[...], g_fpost_ref[...])


def _post_call(x, olat, diff_o, fox_o, mod, lw, tm, state=None):
    B, T, _ = x.shape
    R = mod.shape[1]
    rb = 1 if R == 1 else tm
    decode = state is not None
    mod_map = (lambda c: (lambda b, t: (b, 0, c))) if R == 1 else (lambda c: (lambda b, t: (b, t, c)))
    act = lambda n: pl.BlockSpec((None, tm, n), lambda b, t: (b, t, 0))
    w2 = lambda shape: _const_spec(shape, lambda b, t: (0, 0))
    w3 = lambda shape: _const_spec(shape, lambda b, t: (0, 0, 0))
    in_specs = [act(D_MODEL), act(MLA_HEADS * 128), act(256), act(256),
                pl.BlockSpec((None, rb, D_MODEL), mod_map(2)), pl.BlockSpec((None, rb, D_MODEL), mod_map(3)),
                pl.BlockSpec((None, rb, D_MODEL), mod_map(4)), pl.BlockSpec((None, rb, D_MODEL), mod_map(5)),
                w2((1, D_MODEL)), w2((1, D_MODEL)), w2((1, D_MODEL)),
                w3((4, 256, 128)), w2((D_MODEL, D_MODEL)),
                w3((FFN_NCHUNK, D_MODEL, FFN_CHUNK)), w3((FFN_NCHUNK, D_MODEL, FFN_CHUNK)),
                w3((FFN_NCHUNK, 8, FFN_CHUNK)), w3((FFN_NCHUNK, FFN_CHUNK, D_MODEL))]
    args = [x, olat, diff_o, fox_o, mod, mod, mod, mod, lw["g_attn_post"], lw["g_ffn_pre"], lw["g_ffn_post"],
            lw["wuv_bd"], lw["w_o"], lw["w_a"], lw["w_g"], lw["conv"], lw["w_down"]]
    if decode:
        chunked = pl.BlockSpec((FFN_NCHUNK, tm, FFN_CHUNK), lambda b, t: (0, t, 0))
        in_specs += [chunked] * 2
        args += [s.reshape(T, FFN_NCHUNK, FFN_CHUNK).transpose(1, 0, 2) for s in state]
        out_specs = [act(D_MODEL), chunked]
        out_shape = [jax.ShapeDtypeStruct((B, T, D_MODEL), F32),
                     jax.ShapeDtypeStruct((FFN_NCHUNK, T, FFN_CHUNK), F32)]
        scratch = [pltpu.VMEM((tm, D_MODEL), F32)]
    else:
        out_specs = [act(D_MODEL), pl.BlockSpec((None, FFN_NCHUNK, 8, FFN_CHUNK), lambda b, t: (b, 0, 0, 0))]
        out_shape = [jax.ShapeDtypeStruct((B, T, D_MODEL), F32),
                     jax.ShapeDtypeStruct((B, FFN_NCHUNK, 8, FFN_CHUNK), F32)]
        scratch = [pltpu.VMEM((FFN_NCHUNK, tm + 8, FFN_CHUNK), F32), pltpu.VMEM((tm, D_MODEL), F32)]
    return pl.pallas_call(
        functools.partial(_post_kernel, decode),
        grid=(B, T // tm),
        in_specs=in_specs,
        out_specs=out_specs,
        out_shape=out_shape,
        scratch_shapes=scratch,
        compiler_params=_cparams(("arbitrary", "arbitrary")),
        name="post_decode" if decode else "post",
    )(*args)


def _prep_weights(w_in, mla_w_uq, mla_w_uk, mla_w_uv, w_o, ffn_w_up, ffn_conv_w, ffn_conv_b, ffn_w_down):
    L = N_LAYERS
    zeros = lambda n: jnp.zeros((L, D_MODEL, n), F32)
    o = 0
    seg = {}
    for name, n in (("q_lat", 256), ("kv_lat", 128), ("k_r", 32), ("d_q", 256), ("d_k", 128), ("d_v", 128),
                    ("f_q", 256), ("f_k", 128), ("f_v", 128), ("f_z", 4)):
        seg[name] = w_in[:, :, o:o + n]
        o += n
    k_r = seg["k_r"]
    k_r_sw = jnp.concatenate([k_r[..., 16:], k_r[..., :16]], axis=-1)
    cols = [seg["q_lat"], seg["kv_lat"], k_r, zeros(96), k_r_sw, zeros(96), seg["f_z"], zeros(124)]
    for g in range(2):
        for r in range(2):
            for i in range(2):
                src = seg["d_q"][..., g * 128 + r * 64 + i * 32:g * 128 + r * 64 + (i + 1) * 32]
                lo = g * 64 + i * 32
                cols += [zeros(lo), src, zeros(128 - lo - 32)]
    cols += [seg["d_k"], seg["d_v"]]
    for g in range(2):
        for r in range(2):
            src = seg["f_q"][..., g * 128 + r * 64:g * 128 + (r + 1) * 64]
            cols += [zeros(g * 64), src, zeros(64 - g * 64)]
    cols += [seg["f_k"], seg["f_v"]]
    win = jnp.concatenate([c for c in cols if c.shape[-1] > 0], axis=-1).astype(BF16)

    uq = mla_w_uq.reshape(L, MLA_Q_LORA, MLA_HEADS, MLA_NOPE + MLA_ROPE)
    wq_nope = uq[..., :MLA_NOPE].reshape(L, MLA_Q_LORA, MLA_HEADS * MLA_NOPE).astype(BF16)
    pe = uq[..., MLA_NOPE:]
    wq_pe = pe.reshape(L, MLA_Q_LORA, MLA_HEADS * MLA_ROPE).astype(BF16)
    wq_pe_sw = jnp.concatenate([pe[..., 16:], pe[..., :16]], axis=-1).reshape(
        L, MLA_Q_LORA, MLA_HEADS * MLA_ROPE).astype(BF16)
    eye = jnp.eye(MLA_HEADS, dtype=F32)
    wuk_bd = jnp.einsum("lchn,hk->lhnkc", mla_w_uk, eye).reshape(L, MLA_HEADS * MLA_NOPE,
                                                               MLA_HEADS * MLA_KV_LORA).astype(BF16)
    eye2 = jnp.eye(2, dtype=F32)
    uv = mla_w_uv.reshape(L, MLA_KV_LORA, 4, 2, MLA_V)
    wuv_bd = jnp.einsum("lcjev,ef->ljecfv", uv, eye2).reshape(L, 4, 256, 128).astype(BF16)

    w_a = ffn_w_up[:, :, :D_FF].reshape(L, D_MODEL, FFN_NCHUNK, FFN_CHUNK).transpose(0, 2, 1, 3).astype(BF16)
    w_g = ffn_w_up[:, :, D_FF:].reshape(L, D_MODEL, FFN_NCHUNK, FFN_CHUNK).transpose(0, 2, 1, 3).astype(BF16)
    w_down = ffn_w_down.reshape(L, FFN_NCHUNK, FFN_CHUNK, D_MODEL).astype(BF16)
    conv = jnp.concatenate([ffn_conv_w, ffn_conv_b[:, None, :], jnp.zeros((L, 4, D_FF), F32)], axis=1)
    conv = conv.reshape(L, 8, FFN_NCHUNK, FFN_CHUNK).transpose(0, 2, 1, 3)
    return dict(win=win, wq_nope=wq_nope, wq_pe=wq_pe, wq_pe_sw=wq_pe_sw, wuk_bd=wuk_bd, wuv_bd=wuv_bd,
                w_o=w_o.astype(BF16), w_a=w_a, w_g=w_g, w_down=w_down, conv=conv)


def _rope_tables(pos):
    half = MLA_ROPE // 2
    freqs = ROPE_THETA ** (-jnp.arange(half, dtype=F32) / half)
    ang = pos.astype(F32)[:, None] * freqs
    cos, sin = jnp.cos(ang), jnp.sin(ang)
    ck = jnp.concatenate([cos, cos], axis=-1)
    sk = jnp.concatenate([-sin, sin], axis=-1)
    return jnp.tile(ck, (1, MLA_HEADS)), jnp.tile(sk, (1, MLA_HEADS)), ck, sk


def kernel(x_prompt, x_sample, c_prompt, c_sample, cache_mla, cache_diff_kv, cache_fox_kv, cache_fox_logf, state_conv, page_table, w_ada, b_ada, g_attn_pre, g_attn_post, g_ffn_pre, g_ffn_post, w_in, mla_q_norm, mla_kv_norm, mla_w_uq, mla_w_uk, mla_w_uv, diff_lambda, diff_subln, fox_b_f, w_o, ffn_w_up, ffn_conv_w, ffn_conv_b, ffn_w_down):
    L = N_LAYERS
    B, T, _ = x_prompt.shape
    DB = x_sample.shape[0]
    n_pool = cache_mla.shape[1]
    past_len = page_table.shape[1] * PAGE

    wts = _prep_weights(w_in, mla_w_uq, mla_w_uk, mla_w_uv, w_o, ffn_w_up, ffn_conv_w, ffn_conv_b, ffn_w_down)
    tabs_p = _rope_tables(jnp.arange(T, dtype=jnp.int32))
    tabs_s = _rope_tables(jnp.full((DB,), past_len, jnp.int32))
    tri = (jnp.arange(T)[:, None] <= jnp.arange(T)[None, :]).astype(BF16)
    triu = (jnp.arange(PAGE)[:, None] > jnp.arange(PAGE)[None, :]).astype(BF16)

    cache_mla_t = jnp.transpose(cache_mla, (0, 1, 3, 2))
    cache_diff_t = jnp.transpose(cache_diff_kv, (0, 1, 3, 4, 5, 2)).reshape(L, n_pool, 256, PAGE)
    cache_fox_t = jnp.transpose(cache_fox_kv, (0, 1, 3, 4, 5, 2)).reshape(L, n_pool, 256, PAGE)
    cache_logf_t = jnp.pad(jnp.transpose(cache_fox_logf, (0, 1, 3, 2)), ((0, 0), (0, 0), (0, 4), (0, 0)))
    caches = (cache_mla_t, cache_diff_t, cache_fox_t, cache_logf_t)

    mod_all = _ada_call(jnp.concatenate([c_prompt, c_sample], axis=0), w_ada, b_ada)

    xp = x_prompt
    xs = x_sample.reshape(1, DB, D_MODEL)
    outs = [[] for _ in range(10)]
    for l in range(L):
        lam_init = 0.8 - 0.6 * math.exp(-0.3 * l)
        lw = {k: v[l] for k, v in wts.items()}
        lw.update(g_attn_pre=g_attn_pre[l][None], g_attn_post=g_attn_post[l][None], g_ffn_pre=g_ffn_pre[l][None],
                  g_ffn_post=g_ffn_post[l][None], mla_q_norm=mla_q_norm[l][None], mla_kv_norm=mla_kv_norm[l][None],
                  fox_b_f=fox_b_f[l][None])
        lam_p = diff_lambda[l]
        subln = diff_subln[l].reshape(1, DIFF_HEADS * HEAD_DIM)
        mod_p = mod_all[l, :B].reshape(B, 1, 6 * D_MODEL)
        mod_s = mod_all[l, B:].reshape(1, DB, 6 * D_MODEL)

        mq, mrow, dq, drow, fq, frow, logf = _pre_call(xp, mod_p, lw, tabs_p, PRE_TM)
        logf_t = jnp.pad(jnp.transpose(logf, (0, 2, 1)), ((0, 0), (0, 4), (0, 0)))
        f_cum = _cumsum_call(logf_t, tri)
        f_cum_t = jnp.transpose(f_cum[:, :FOX_HEADS], (0, 2, 1))
        f_cum_k = f_cum.reshape(B, 8, T // ATT_TK, ATT_TK).transpose(0, 2, 1, 3)
        olat, diff_o, fox_o = _attn_call(mq, dq, fq, mrow, drow, frow, f_cum_k, f_cum_t, lam_p, subln, lam_init)
        xp, tail = _post_call(xp, olat, diff_o, fox_o, mod_p, lw, POST_TM)
        conv_p = tail[:, :, 6:8, :].transpose(0, 2, 1, 3).reshape(B, 2, D_FF)

        mq_s, mrow_s, dq_s, drow_s, fq_s, frow_s, logf_s = _pre_call(xs, mod_s, lw, tabs_s, DB)
        fq_s8 = jnp.pad(jnp.transpose(fq_s[0], (1, 0, 2)), ((0, 0), (0, 4), (0, 0)))
        lf_s8 = jnp.pad(logf_s[0], ((0, 0), (0, 4)))[:, :, None]
        m_o, d_o, f_o = _decode_call(
            page_table, l, caches, jnp.transpose(mq_s[0], (1, 0, 2)), jnp.transpose(dq_s[0], (1, 0, 2)), fq_s8,
            mrow_s.reshape(DB, 1, MLA_ROW), drow_s.reshape(DB, 1, 256), frow_s.reshape(DB, 1, 256), lf_s8, triu)
        diff_s, fox_s = _decode_finish_call(jnp.transpose(d_o, (1, 0, 2)), jnp.transpose(f_o[:, :4], (1, 0, 2)),
                                            lam_p, subln, lam_init)
        olat_s = m_o.reshape(1, DB, MLA_HEADS * 128).astype(BF16)
        st = state_conv[l]
        xs, a_s = _post_call(xs, olat_s, diff_s[None], fox_s[None], mod_s, lw, DB, state=(st[:, 0], st[:, 1]))
        conv_s = jnp.stack([st[:, 1], a_s.transpose(1, 0, 2).reshape(DB, D_FF)], axis=1)

        for i, v in enumerate((mrow, mrow_s.reshape(DB, 1, MLA_ROW), drow.reshape(B, T, 2, 2, HEAD_DIM),
                               drow_s.reshape(DB, 1, 2, 2, HEAD_DIM), frow.reshape(B, T, 2, 2, HEAD_DIM),
                               frow_s.reshape(DB, 1, 2, 2, HEAD_DIM), logf, logf_s.reshape(DB, 1, FOX_HEADS),
                               conv_p, conv_s)):
            outs[i].append(v)

    return (xp, xs.reshape(DB, 1, D_MODEL)) + tuple(jnp.stack(o) for o in outs)
```

```python
import functools
import math

import jax
import jax.numpy as jnp
import numpy as np
from jax import lax
from jax.experimental import pallas as pl
from jax.experimental.pallas import tpu as pltpu

F32 = jnp.float32
BF16 = jnp.bfloat16

D_MODEL = 1024
D_FF = 2816
N_LAYERS = 4
MLA_HEADS = 8
MLA_Q_LORA = 256
MLA_KV_LORA = 128
MLA_NOPE = 64
MLA_ROPE = 32
MLA_V = 64
MLA_ROW = MLA_KV_LORA + MLA_ROPE
DIFF_HEADS = 4
DIFF_HALF = 32
FOX_HEADS = 4
HEAD_DIM = 64
ROPE_THETA = 10000.0
RMS_EPS = 1e-6
NEG_INF = -1e30
PAGE = 128

C_QLAT = 0
C_KVLAT = 256
C_KR = 384
C_KRS = 512
C_FZ = 640
C_DQ = 768
C_DROW = 1792
C_FQ = 2048
C_FROW = 2560
C_TOTAL = 2816

FFN_CHUNK = 256
FFN_NCHUNK = D_FF // FFN_CHUNK

PRE_TM = 512
POST_TM = 512
ATT_TQ = 128
ATT_TK = 512
DEC_PAGES = 32
LOG2E = math.log2(math.e)
VMEM_LIMIT = 56 * 1024 * 1024


def _cparams(sem):
    return pltpu.CompilerParams(dimension_semantics=sem, vmem_limit_bytes=VMEM_LIMIT)


def _const_spec(shape, index_map):
    return pl.BlockSpec(shape, index_map, pipeline_mode=pl.Buffered(1))


def _rms(x, g):
    return x * lax.rsqrt(jnp.mean(x * x, axis=-1, keepdims=True) + RMS_EPS) * g


def _dot(a, b):
    return jnp.dot(a, b, preferred_element_type=F32)


def _dot_nt(a, b):
    return lax.dot_general(a, b, (((1,), (1,)), ((), ())), preferred_element_type=F32)


def _ada_kernel(c_ref, w_ref, b_ref, o_ref):
    c = c_ref[...]
    s = (c * jax.nn.sigmoid(c)).astype(BF16)
    o_ref[...] = _dot(s, w_ref[...].astype(BF16)) + b_ref[...]


def _ada_call(c_all, w_ada, b_ada):
    n = c_all.shape[0]
    tn = 1536
    return pl.pallas_call(
        _ada_kernel,
        grid=(N_LAYERS, 6 * D_MODEL // tn),
        in_specs=[pl.BlockSpec((n, D_MODEL), lambda l, j: (0, 0)),
                  pl.BlockSpec((None, D_MODEL, tn), lambda l, j: (l, 0, j)),
                  pl.BlockSpec((None, 1, tn), lambda l, j: (l, 0, j))],
        out_specs=pl.BlockSpec((None, n, tn), lambda l, j: (l, 0, j)),
        out_shape=jax.ShapeDtypeStruct((N_LAYERS, n, 6 * D_MODEL), F32),
        compiler_params=_cparams(("arbitrary", "arbitrary")),
        name="ada",
    )(c_all, w_ada, b_ada.reshape(N_LAYERS, 1, 6 * D_MODEL))


def _pre_kernel(x_ref, shift_ref, scale_ref, g_ref, win_ref, qn_ref, kvn_ref, wqn_ref, wqp_ref, wqps_ref,
                wuk_ref, bf_ref, cq_ref, sq_ref, ck_ref, sk_ref,
                mq_ref, mrow_ref, dq_ref, drow_ref, fq_ref, frow_ref, logf_ref):
    x = x_ref[...]
    h = _rms(x, g_ref[...]) * (1.0 + scale_ref[...]) + shift_ref[...]
    z = _dot(h.astype(BF16), win_ref[...])

    qn = _rms(z[:, C_QLAT:C_QLAT + MLA_Q_LORA], qn_ref[...]).astype(BF16)
    q_nope = _dot(qn, wqn_ref[...])
    q_pe = _dot(qn, wqp_ref[...]) * cq_ref[...] + _dot(qn, wqps_ref[...]) * sq_ref[...]
    q_abs = _dot(q_nope.astype(BF16), wuk_ref[...])
    for hh in range(MLA_HEADS):
        mq_ref[hh, :, 0:MLA_KV_LORA] = q_abs[:, hh * 128:(hh + 1) * 128].astype(BF16)
        mq_ref[hh, :, MLA_KV_LORA:MLA_ROW] = q_pe[:, hh * MLA_ROPE:(hh + 1) * MLA_ROPE].astype(BF16)

    mrow_ref[:, 0:MLA_KV_LORA] = _rms(z[:, C_KVLAT:C_KVLAT + MLA_KV_LORA], kvn_ref[...])
    mrow_ref[:, MLA_KV_LORA:MLA_ROW] = (z[:, C_KR:C_KR + MLA_ROPE] * ck_ref[...]
                                        + z[:, C_KRS:C_KRS + MLA_ROPE] * sk_ref[...])

    for j in range(8):
        dq_ref[j] = z[:, C_DQ + j * 128:C_DQ + (j + 1) * 128].astype(BF16)
    drow_ref[...] = z[:, C_DROW:C_DROW + 256]
    for j in range(4):
        fq_ref[j] = z[:, C_FQ + j * 128:C_FQ + (j + 1) * 128].astype(BF16)
    frow_ref[...] = z[:, C_FROW:C_FROW + 256]

    fz = z[:, C_FZ:C_FZ + FOX_HEADS] + bf_ref[...]
    logf_ref[...] = jnp.minimum(fz, 0.0) - jnp.log1p(jnp.exp(-jnp.abs(fz)))


def _pre_call(x, mod, lw, tabs, tm):
    B, T, _ = x.shape
    R = mod.shape[1]
    rb = 1 if R == 1 else tm
    mod_map = (lambda c: (lambda b, t: (b, 0, c))) if R == 1 else (lambda c: (lambda b, t: (b, t, c)))
    cq, sq, ck, sk = tabs
    w2 = lambda shape: _const_spec(shape, lambda b, t: (0, 0))
    row = lambda n: pl.BlockSpec((tm, n), lambda b, t: (t, 0))
    outs = pl.pallas_call(
        _pre_kernel,
        grid=(B, T // tm),
        in_specs=[pl.BlockSpec((None, tm, D_MODEL), lambda b, t: (b, t, 0)),
                  pl.BlockSpec((None, rb, D_MODEL), mod_map(0)),
                  pl.BlockSpec((None, rb, D_MODEL), mod_map(1)),
                  w2((1, D_MODEL)), w2((D_MODEL, C_TOTAL)), w2((1, MLA_Q_LORA)), w2((1, MLA_KV_LORA)),
                  w2((MLA_Q_LORA, 512)), w2((MLA_Q_LORA, 256)), w2((MLA_Q_LORA, 256)), w2((512, 1024)),
                  w2((1, FOX_HEADS)), row(256), row(256), row(MLA_ROPE), row(MLA_ROPE)],
        out_specs=[pl.BlockSpec((None, 8, tm, MLA_ROW), lambda b, t: (b, 0, t, 0)),
                   pl.BlockSpec((None, tm, MLA_ROW), lambda b, t: (b, t, 0)),
                   pl.BlockSpec((None, 8, tm, 128), lambda b, t: (b, 0, t, 0)),
                   pl.BlockSpec((None, tm, 256), lambda b, t: (b, t, 0)),
                   pl.BlockSpec((None, 4, tm, 128), lambda b, t: (b, 0, t, 0)),
                   pl.BlockSpec((None, tm, 256), lambda b, t: (b, t, 0)),
                   pl.BlockSpec((None, tm, FOX_HEADS), lambda b, t: (b, t, 0))],
        out_shape=[jax.ShapeDtypeStruct((B, 8, T, MLA_ROW), BF16),
                   jax.ShapeDtypeStruct((B, T, MLA_ROW), F32),
                   jax.ShapeDtypeStruct((B, 8, T, 128), BF16),
                   jax.ShapeDtypeStruct((B, T, 256), F32),
                   jax.ShapeDtypeStruct((B, 4, T, 128), BF16),
                   jax.ShapeDtypeStruct((B, T, 256), F32),
                   jax.ShapeDtypeStruct((B, T, FOX_HEADS), F32)],
        compiler_params=_cparams(("arbitrary", "arbitrary")),
        name="pre",
    )(x, mod, mod, lw["g_attn_pre"], lw["win"], lw["mla_q_norm"], lw["mla_kv_norm"], lw["wq_nope"],
      lw["wq_pe"], lw["wq_pe_sw"], lw["wuk_bd"], lw["fox_b_f"], cq, sq, ck, sk)
    return outs


def _diag_mask(nblk, tq):
    rows = lax.broadcasted_iota(jnp.int32, (nblk * tq, tq), 0) & (tq - 1)
    cols = lax.broadcasted_iota(jnp.int32, (nblk * tq, tq), 1)
    return cols <= rows


def _alibi_slope(head):
    return jnp.where(head == 0, 2.0 ** -2, jnp.where(head == 1, 2.0 ** -4,
                                                     jnp.where(head == 2, 2.0 ** -6, 2.0 ** -8))).astype(F32)


def _lambda_value(lam_ref, lam_init):
    lv = lam_ref[...]
    a = jnp.sum(lv[0:1, :] * lv[1:2, :], axis=-1, keepdims=True)
    b = jnp.sum(lv[2:3, :] * lv[3:4, :], axis=-1, keepdims=True)
    return jnp.exp(a) - jnp.exp(b) + lam_init


def _diff_combine(blocks, lam, subln, lam_init):
    outs = []
    for g in range(2):
        for r in range(2):
            o = blocks[g * 4 + r * 2] - lam * blocks[g * 4 + r * 2 + 1]
            o = o[:, g * HEAD_DIM:(g + 1) * HEAD_DIM]
            hh = g * 2 + r
            outs.append(_rms(o, subln[:, hh * HEAD_DIM:(hh + 1) * HEAD_DIM]) * (1.0 - lam_init))
    return jnp.concatenate(outs, axis=1)


def _fox_select(blocks):
    return jnp.concatenate([blocks[g * 2 + r][:, g * HEAD_DIM:(g + 1) * HEAD_DIM]
                            for g in range(2) for r in range(2)], axis=1)


def _softmax_step(chunks, v_bf, m_ref, l_ref, acc_ref):
    mx = chunks[0]
    for c in chunks[1:]:
        mx = jnp.maximum(mx, c)
    m_prev = m_ref[...]
    m_new = jnp.maximum(m_prev, jnp.max(mx, axis=-1, keepdims=True))
    alpha = jnp.exp2(m_prev - m_new)
    ps = [jnp.exp2(c - m_new) for c in chunks]
    lsum = ps[0]
    for p in ps[1:]:
        lsum = lsum + p
    l_ref[...] = alpha * l_ref[...] + lsum
    p_bf = jnp.concatenate([p.astype(BF16) for p in ps], axis=1) if len(ps) > 1 else ps[0].astype(BF16)
    acc_ref[...] = alpha * acc_ref[...] + _dot(p_bf, v_bf)
    m_ref[...] = m_new


def _attn_kernel(lam_init, mq_ref, dq_ref, fq_ref, mkv_ref, dkv_ref, fkv_ref, fk_ref, fqc_ref, lam_ref, subln_ref,
                 mo_ref, do_ref, fo_ref, mm, ml, macc, dm, dl, dacc, fm, fl, facc, sb_sc, sl_sc):
    tq, tk = ATT_TQ, ATT_TK
    per_tile = tk // tq
    qi = pl.program_id(1)
    m_c = LOG2E / math.sqrt(MLA_NOPE + MLA_ROPE)
    d_c = LOG2E / math.sqrt(DIFF_HALF)
    f_c = LOG2E / math.sqrt(HEAD_DIM)

    @pl.when((pl.program_id(0) == 0) & (qi == 0))
    def _():
        rows = lax.broadcasted_iota(jnp.int32, (8 * tq, tk), 0)
        cols = lax.broadcasted_iota(jnp.int32, (8 * tq, tk), 1)
        slope = _alibi_slope(rows >> (int(math.log2(tq)) + 1)) * LOG2E
        sb_sc[...] = slope * ((rows & (tq - 1)) - cols).astype(F32)
        sl_sc[...] = slope[:, 0:128]

    for m_ref, l_ref, acc_ref in ((mm, ml, macc), (dm, dl, dacc), (fm, fl, facc)):
        m_ref[...] = jnp.full(m_ref.shape, NEG_INF, F32)
        l_ref[...] = jnp.zeros(l_ref.shape, F32)
        acc_ref[...] = jnp.zeros(acc_ref.shape, F32)

    mq = mq_ref[...].reshape(MLA_HEADS * tq, MLA_ROW)
    dq = dq_ref[...].reshape(8 * tq, 128)
    fq = fq_ref[...].reshape(4 * tq, 128)
    fqc = fqc_ref[...] * LOG2E
    fq_bias = jnp.concatenate([jnp.broadcast_to(fqc[:, hh:hh + 1], (tq, 128)) for hh in range(FOX_HEADS)], axis=0)

    def tile(kt, width, diag, off):
        nc = width // 128
        k0 = pl.multiple_of(kt * tk, tk)
        sl = lambda s, j: s[:, j * 128:(j + 1) * 128]

        kb = mkv_ref[pl.ds(k0, width), :].astype(BF16)
        s = _dot_nt(mq, kb) * m_c
        chunks = [sl(s, j) for j in range(nc)]
        if diag:
            chunks[-1] = jnp.where(_diag_mask(MLA_HEADS, tq), chunks[-1], NEG_INF)
        _softmax_step(chunks, kb[:, 0:MLA_KV_LORA], mm, ml, macc)

        kb = dkv_ref[pl.ds(k0, width), 0:128].astype(BF16)
        vb = dkv_ref[pl.ds(k0, width), 128:256].astype(BF16)
        s = _dot_nt(dq, kb) * d_c
        slope_off = sl_sc[...] * off
        chunks = [(sl(s, j) - sb_sc[:, j * 128:(j + 1) * 128]) - slope_off for j in range(nc)]
        if diag:
            chunks[-1] = jnp.where(_diag_mask(8, tq), chunks[-1], NEG_INF)
        _softmax_step(chunks, vb, dm, dl, dacc)

        kb = fkv_ref[pl.ds(k0, width), 0:128].astype(BF16)
        vb = fkv_ref[pl.ds(k0, width), 128:256].astype(BF16)
        s = _dot_nt(fq, kb) * f_c
        fk = fk_ref[kt] * LOG2E
        chunks = []
        for j in range(nc):
            fk_j = jnp.concatenate([jnp.broadcast_to(fk[hh:hh + 1, j * 128:(j + 1) * 128], (tq, 128))
                                    for hh in range(FOX_HEADS)], axis=0)
            chunks.append((sl(s, j) + fq_bias) - fk_j)
        if diag:
            chunks[-1] = jnp.where(_diag_mask(4, tq), chunks[-1], NEG_INF)
        _softmax_step(chunks, vb, fm, fl, facc)

    n_full = qi // per_tile
    rem = qi - n_full * per_tile

    def body(kt, c):
        tile(kt, tk, False, (qi * tq - kt * tk).astype(F32))
        return c

    lax.fori_loop(0, n_full, body, 0)
    for r in range(per_tile):
        @pl.when(rem == r)
        def _():
            tile(n_full, (r + 1) * tq, True, float(r * tq))

    def finish(l_ref, acc_ref):
        return acc_ref[...] / jnp.sum(l_ref[...], axis=-1, keepdims=True)

    o = finish(ml, macc)
    for hh in range(MLA_HEADS):
        mo_ref[:, hh * 128:(hh + 1) * 128] = o[hh * tq:(hh + 1) * tq].astype(BF16)
    o = finish(dl, dacc)
    lam = _lambda_value(lam_ref, lam_init)
    do_ref[...] = _diff_combine([o[j * tq:(j + 1) * tq] for j in range(8)], lam, subln_ref[...],
                                lam_init).astype(BF16)
    o = finish(fl, facc)
    fo_ref[...] = _fox_select([o[j * tq:(j + 1) * tq] for j in range(4)]).astype(BF16)


def _attn_call(mq, dq, fq, mrow, drow, frow, f_cum_k, f_cum_t, lam_p, subln, lam_init):
    B, _, T, _ = mq.shape
    tq, tk = ATT_TQ, ATT_TK
    qblk = lambda n, d: pl.BlockSpec((None, n, tq, d), lambda b, i: (b, 0, i, 0))
    full = lambda d: pl.BlockSpec((None, T, d), lambda b, i: (b, 0, 0))
    oblk = lambda d: pl.BlockSpec((None, tq, d), lambda b, i: (b, i, 0))
    stat = lambda n: [pltpu.VMEM((n * tq, 128), F32)] * 3
    return pl.pallas_call(
        functools.partial(_attn_kernel, lam_init),
        grid=(B, T // tq),
        in_specs=[qblk(MLA_HEADS, MLA_ROW), qblk(8, 128), qblk(4, 128), full(MLA_ROW), full(256), full(256),
                  pl.BlockSpec((None, T // tk, 8, tk), lambda b, i: (b, 0, 0, 0)),
                  pl.BlockSpec((None, tq, FOX_HEADS), lambda b, i: (b, i, 0)),
                  pl.BlockSpec((4, DIFF_HALF), lambda b, i: (0, 0)),
                  pl.BlockSpec((1, 256), lambda b, i: (0, 0))],
        out_specs=[oblk(MLA_HEADS * 128), oblk(256), oblk(256)],
        out_shape=[jax.ShapeDtypeStruct((B, T, MLA_HEADS * 128), BF16),
                   jax.ShapeDtypeStruct((B, T, 256), BF16), jax.ShapeDtypeStruct((B, T, 256), BF16)],
        scratch_shapes=stat(MLA_HEADS) + stat(8) + stat(4) + [pltpu.VMEM((8 * tq, tk), F32),
                                                              pltpu.VMEM((8 * tq, 128), F32)],
        compiler_params=_cparams(("arbitrary", "arbitrary")),
        name="prefill_attn",
    )(mq, dq, fq, mrow, drow, frow, f_cum_k, f_cum_t, lam_p, subln)


def _split3(x):
    hi = x.astype(BF16)
    r = x - hi.astype(F32)
    mid = r.astype(BF16)
    lo = (r - mid.astype(F32)).astype(BF16)
    return hi, mid, lo


def _cumsum_kernel(x_ref, tri_ref, o_ref):
    hi, mid, lo = _split3(x_ref[...])
    tri = tri_ref[...]
    o_ref[...] = (_dot(hi, tri) + _dot(mid, tri)) + _dot(lo, tri)


def _cumsum_call(x, tri):
    B, R, T = x.shape
    return pl.pallas_call(
        _cumsum_kernel,
        grid=(B,),
        in_specs=[pl.BlockSpec((None, R, T), lambda b: (b, 0, 0)),
                  _const_spec((T, T), lambda b: (0, 0))],
        out_specs=pl.BlockSpec((None, R, T), lambda b: (b, 0, 0)),
        out_shape=jax.ShapeDtypeStruct((B, R, T), F32),
        compiler_params=_cparams(("arbitrary",)),
        name="logf_cumsum",
    )(x, tri)


def _decode_kernel(past_len, layer, nseq, nch, pt_ref, mq_ref, dq_ref, fq_ref, mnew_ref, dnew_ref, fnew_ref,
                   lfnew_ref, triu_ref, mla_hbm, diff_hbm, fox_hbm, logf_hbm, mo_ref, do_ref, fo_ref,
                   mla_buf, diff_buf, fox_buf, logf_buf, sem, mm, ml, macc, dm, dl, dacc, fm, fl, facc, carry):
    np_ = DEC_PAGES
    b = pl.program_id(0)
    c = pl.program_id(1)
    width = np_ * PAGE
    step = b * nch + c
    slot = step % 2

    def page_copies(bb, cc, sl, j):
        page = pt_ref[bb, (nch - 1 - cc) * np_ + j]
        pairs = ((mla_hbm, mla_buf), (diff_hbm, diff_buf), (fox_hbm, fox_buf), (logf_hbm, logf_buf))
        return [pltpu.make_async_copy(hbm.at[layer, page], buf.at[sl, j], sem.at[sl, k])
                for k, (hbm, buf) in enumerate(pairs)]

    def start_step(bb, cc, sl):
        def body(j, carry_):
            for cp in page_copies(bb, cc, sl, j):
                cp.start()
            return carry_
        lax.fori_loop(0, np_, body, 0)

    @pl.when(step == 0)
    def _():
        start_step(b, c, slot)

    nxt = step + 1

    @pl.when(nxt < nseq * nch)
    def _():
        start_step(nxt // nch, nxt % nch, 1 - slot)

    def wait_body(j, carry_):
        for cp in page_copies(b, c, slot, j):
            cp.wait()
        return carry_
    lax.fori_loop(0, np_, wait_body, 0)

    mla_pages = [mla_buf.at[slot, j] for j in range(np_)]
    diff_pages = [diff_buf.at[slot, j] for j in range(np_)]
    fox_pages = [fox_buf.at[slot, j] for j in range(np_)]
    logf_pages = [logf_buf.at[slot, j] for j in range(np_)]

    mq = mq_ref[...]
    dq = dq_ref[...]
    fq = fq_ref[...]
    m_scale = 1.0 / math.sqrt(MLA_NOPE + MLA_ROPE)
    d_scale = 1.0 / math.sqrt(DIFF_HALF)
    f_scale = 1.0 / math.sqrt(HEAD_DIM)

    @pl.when(c == 0)
    def _():
        def self_init(q, k_new, v_new, scale, m_sc, l_sc, acc_sc):
            kb = k_new.astype(BF16).astype(F32)
            m_sc[...] = jnp.sum(q.astype(F32) * kb, axis=-1, keepdims=True) * scale
            l_sc[...] = jnp.ones(l_sc.shape, F32)
            acc_sc[...] = jnp.broadcast_to(v_new.astype(BF16).astype(F32), acc_sc.shape)

        mnew = mnew_ref[...]
        self_init(mq, mnew, mnew[:, 0:MLA_KV_LORA], m_scale, mm, ml, macc)
        dnew = dnew_ref[...]
        self_init(dq, dnew[:, 0:128], dnew[:, 128:256], d_scale, dm, dl, dacc)
        fnew = fnew_ref[...]
        self_init(fq, fnew[:, 0:128], fnew[:, 128:256], f_scale, fm, fl, facc)
        carry[...] = lfnew_ref[...]

    def update(s, vt_bf, m_sc, l_sc, acc_sc):
        m_prev = m_sc[...]
        m_new = jnp.maximum(m_prev, jnp.max(s, axis=-1, keepdims=True))
        alpha = jnp.exp(m_prev - m_new)
        p = jnp.exp(s - m_new)
        l_sc[...] = alpha * l_sc[...] + jnp.sum(p, axis=-1, keepdims=True)
        acc_sc[...] = alpha * acc_sc[...] + _dot_nt(p.astype(BF16), vt_bf)
        m_sc[...] = m_new

    mla_t = jnp.concatenate([r[...].astype(BF16) for r in mla_pages], axis=1)
    update(_dot(mq, mla_t) * m_scale, mla_t[0:MLA_KV_LORA, :], mm, ml, macc)

    first_pos = (nch - 1 - c) * width
    diff_k = jnp.concatenate([r[0:128, :].astype(BF16) for r in diff_pages], axis=1)
    diff_v = jnp.concatenate([r[128:256, :].astype(BF16) for r in diff_pages], axis=1)
    kpos = first_pos + lax.broadcasted_iota(jnp.int32, (8, width), 1)
    slope = _alibi_slope(lax.broadcasted_iota(jnp.int32, (8, width), 0) >> 1)
    s_d = _dot(dq, diff_k) * d_scale - slope * (past_len - kpos).astype(F32)
    update(s_d, diff_v, dm, dl, dacc)

    fox_k = jnp.concatenate([r[0:128, :].astype(BF16) for r in fox_pages], axis=1)
    fox_v = jnp.concatenate([r[128:256, :].astype(BF16) for r in fox_pages], axis=1)
    lf = [r[...] for r in logf_pages]
    hi, mid, lo = _split3(jnp.concatenate(lf, axis=0))
    triu = triu_ref[...]
    within = (_dot(hi, triu) + _dot(mid, triu)) + _dot(lo, triu)
    run = carry[...]
    bias = [None] * np_
    for j in range(np_ - 1, -1, -1):
        bias[j] = within[j * 8:(j + 1) * 8] + run
        run = run + jnp.sum(lf[j], axis=-1, keepdims=True)
    carry[...] = run
    s_f = _dot(fq, fox_k) * f_scale + jnp.concatenate(bias, axis=1)
    update(s_f, fox_v, fm, fl, facc)

    @pl.when(c == nch - 1)
    def _():
        mo_ref[...] = macc[...] / ml[...]
        do_ref[...] = dacc[...] / dl[...]
        fo_ref[...] = facc[...] / fl[...]


def _decode_call(page_table, layer, caches, mq, dq, fq, mnew, dnew, fnew, lfnew, triu):
    cache_mla_t, cache_diff_t, cache_fox_t, cache_logf_t = caches
    nseq, npages = page_table.shape
    past_len = npages * PAGE
    nch = npages // DEC_PAGES

    seq3 = lambda r, n: pl.BlockSpec((None, r, n), lambda b, c, pt: (b, 0, 0))
    in_specs = [seq3(8, MLA_ROW), seq3(8, 128), seq3(8, 128), seq3(1, MLA_ROW), seq3(1, 256), seq3(1, 256),
                seq3(8, 1), pl.BlockSpec((PAGE, PAGE), lambda b, c, pt: (0, 0))]
    in_specs += [pl.BlockSpec(memory_space=pl.ANY)] * 4
    small = lambda: [pltpu.VMEM((8, 1), F32), pltpu.VMEM((8, 1), F32), pltpu.VMEM((8, 128), F32)]
    page_bufs = [pltpu.VMEM((2, DEC_PAGES, rows, PAGE), F32) for rows in (MLA_ROW, 256, 256, 8)]
    grid_spec = pltpu.PrefetchScalarGridSpec(
        num_scalar_prefetch=1,
        grid=(nseq, nch),
        in_specs=in_specs,
        out_specs=[seq3(8, 128), seq3(8, 128), seq3(8, 128)],
        scratch_shapes=page_bufs + [pltpu.SemaphoreType.DMA((2, 4))] + small() + small() + small()
        + [pltpu.VMEM((8, 1), F32)],
    )
    return pl.pallas_call(
        functools.partial(_decode_kernel, past_len, layer, nseq, nch),
        grid_spec=grid_spec,
        out_shape=[jax.ShapeDtypeStruct((nseq, 8, 128), F32)] * 3,
        compiler_params=_cparams(("arbitrary", "arbitrary")),
        name="decode_attn",
    )(page_table, mq, dq, fq, mnew, dnew, fnew, lfnew, triu,
      cache_mla_t, cache_diff_t, cache_fox_t, cache_logf_t)


def _decode_finish_kernel(lam_init, d_ref, f_ref, lam_ref, subln_ref, do_ref, fo_ref):
    lam = _lambda_value(lam_ref, lam_init)
    do_ref[...] = _diff_combine([d_ref[j] for j in range(8)], lam, subln_ref[...], lam_init).astype(BF16)
    fo_ref[...] = _fox_select([f_ref[j] for j in range(4)]).astype(BF16)


def _decode_finish_call(d_blocks, f_blocks, lam_p, subln, lam_init):
    n = d_blocks.shape[1]
    return pl.pallas_call(
        functools.partial(_decode_finish_kernel, lam_init),
        out_shape=[jax.ShapeDtypeStruct((n, 256), BF16)] * 2,
        name="decode_finish",
    )(d_blocks, f_blocks, lam_p, subln)


def _post_kernel(decode, x_ref, olat_ref, diff_ref, fox_ref, gate_a_ref, shift_f_ref, scale_f_ref, gate_f_ref,
                 g_post_ref, g_fpre_ref, g_fpost_ref, wuv_ref, wo_ref, wa_ref, wg_ref, cw_ref, wd_ref, *rest):
    if decode:
        s0_ref, s1_ref, xo_ref, a_out_ref, y_sc = rest
    else:
        xo_ref, tail_ref, a_sc, y_sc = rest
    tm = x_ref.shape[0]
    olat = olat_ref[...]
    mla = [_dot(olat[:, j * 256:(j + 1) * 256], wuv_ref[j]).astype(BF16) for j in range(4)]
    att = jnp.concatenate(mla + [diff_ref[...], fox_ref[...]], axis=1)
    mix = _dot(att, wo_ref[...])
    x1 = x_ref[...] + gate_a_ref[...] * _rms(mix, g_post_ref[...])
    h = (_rms(x1, g_fpre_ref[...]) * (1.0 + scale_f_ref[...]) + shift_f_ref[...]).astype(BF16)

    if not decode:
        @pl.when(pl.program_id(1) == 0)
        def _():
            a_sc[:, 0:8, :] = jnp.zeros((FFN_NCHUNK, 8, FFN_CHUNK), F32)
    y_sc[...] = jnp.zeros(y_sc.shape, F32)

    def chunk(ci, carry):
        a = _dot(h, wa_ref[ci])
        g = _dot(h, wg_ref[ci])
        cw = cw_ref[ci]
        if decode:
            a_out_ref[ci] = a
            conv = cw[3:4] + cw[0:1] * s0_ref[ci] + cw[1:2] * s1_ref[ci] + cw[2:3] * a
        else:
            a_sc[ci, 8:8 + tm, :] = a
            conv = (cw[3:4] + cw[0:1] * a_sc[ci, 6:6 + tm, :] + cw[1:2] * a_sc[ci, 7:7 + tm, :] + cw[2:3] * a)
            last = a_sc[ci, tm:tm + 8, :]
            a_sc[ci, 0:8, :] = last
            tail_ref[ci] = last
        gelu = 0.5 * conv * (1.0 + jnp.tanh(math.sqrt(2.0 / math.pi) * (conv + 0.044715 * (conv * conv * conv))))
        y_sc[...] += _dot((gelu * g).astype(BF16), wd_ref[ci])
        return carry

    lax.fori_loop(0, FFN_NCHUNK, chunk, 0)
    xo_ref[...] = x1 + gate_f_ref[...] * _rms(y_sc[...], g_fpost_ref[...])


def _post_call(x, olat, diff_o, fox_o, mod, lw, tm, state=None):
    B, T, _ = x.shape
    R = mod.shape[1]
    rb = 1 if R == 1 else tm
    decode = state is not None
    mod_map = (lambda c: (lambda b, t: (b, 0, c))) if R == 1 else (lambda c: (lambda b, t: (b, t, c)))
    act = lambda n: pl.BlockSpec((None, tm, n), lambda b, t: (b, t, 0))
    w2 = lambda shape: _const_spec(shape, lambda b, t: (0, 0))
    w3 = lambda shape: _const_spec(shape, lambda b, t: (0, 0, 0))
    in_specs = [act(D_MODEL), act(MLA_HEADS * 128), act(256), act(256),
                pl.BlockSpec((None, rb, D_MODEL), mod_map(2)), pl.BlockSpec((None, rb, D_MODEL), mod_map(3)),
                pl.BlockSpec((None, rb, D_MODEL), mod_map(4)), pl.BlockSpec((None, rb, D_MODEL), mod_map(5)),
                w2((1, D_MODEL)), w2((1, D_MODEL)), w2((1, D_MODEL)),
                w3((4, 256, 128)), w2((D_MODEL, D_MODEL)),
                w3((FFN_NCHUNK, D_MODEL, FFN_CHUNK)), w3((FFN_NCHUNK, D_MODEL, FFN_CHUNK)),
                w3((FFN_NCHUNK, 8, FFN_CHUNK)), w3((FFN_NCHUNK, FFN_CHUNK, D_MODEL))]
    args = [x, olat, diff_o, fox_o, mod, mod, mod, mod, lw["g_attn_post"], lw["g_ffn_pre"], lw["g_ffn_post"],
            lw["wuv_bd"], lw["w_o"], lw["w_a"], lw["w_g"], lw["conv"], lw["w_down"]]
    if decode:
        chunked = pl.BlockSpec((FFN_NCHUNK, tm, FFN_CHUNK), lambda b, t: (0, t, 0))
        in_specs += [chunked] * 2
        args += [s.reshape(T, FFN_NCHUNK, FFN_CHUNK).transpose(1, 0, 2) for s in state]
        out_specs = [act(D_MODEL), chunked]
        out_shape = [jax.ShapeDtypeStruct((B, T, D_MODEL), F32),
                     jax.ShapeDtypeStruct((FFN_NCHUNK, T, FFN_CHUNK), F32)]
        scratch = [pltpu.VMEM((tm, D_MODEL), F32)]
    else:
        out_specs = [act(D_MODEL), pl.BlockSpec((None, FFN_NCHUNK, 8, FFN_CHUNK), lambda b, t: (b, 0, 0, 0))]
        out_shape = [jax.ShapeDtypeStruct((B, T, D_MODEL), F32),
                     jax.ShapeDtypeStruct((B, FFN_NCHUNK, 8, FFN_CHUNK), F32)]
        scratch = [pltpu.VMEM((FFN_NCHUNK, tm + 8, FFN_CHUNK), F32), pltpu.VMEM((tm, D_MODEL), F32)]
    return pl.pallas_call(
        functools.partial(_post_kernel, decode),
        grid=(B, T // tm),
        in_specs=in_specs,
        out_specs=out_specs,
        out_shape=out_shape,
        scratch_shapes=scratch,
        compiler_params=_cparams(("arbitrary", "arbitrary")),
        name="post_decode" if decode else "post",
    )(*args)


def _prep_weights(w_in, mla_w_uq, mla_w_uk, mla_w_uv, w_o, ffn_w_up, ffn_conv_w, ffn_conv_b, ffn_w_down):
    L = N_LAYERS
    zeros = lambda n: jnp.zeros((L, D_MODEL, n), F32)
    o = 0
    seg = {}
    for name, n in (("q_lat", 256), ("kv_lat", 128), ("k_r", 32), ("d_q", 256), ("d_k", 128), ("d_v", 128),
                    ("f_q", 256), ("f_k", 128), ("f_v", 128), ("f_z", 4)):
        seg[name] = w_in[:, :, o:o + n]
        o += n
    k_r = seg["k_r"]
    k_r_sw = jnp.concatenate([k_r[..., 16:], k_r[..., :16]], axis=-1)
    cols = [seg["q_lat"], seg["kv_lat"], k_r, zeros(96), k_r_sw, zeros(96), seg["f_z"], zeros(124)]
    for g in range(2):
        for r in range(2):
            for i in range(2):
                src = seg["d_q"][..., g * 128 + r * 64 + i * 32:g * 128 + r * 64 + (i + 1) * 32]
                lo = g * 64 + i * 32
                cols += [zeros(lo), src, zeros(128 - lo - 32)]
    cols += [seg["d_k"], seg["d_v"]]
    for g in range(2):
        for r in range(2):
            src = seg["f_q"][..., g * 128 + r * 64:g * 128 + (r + 1) * 64]
            cols += [zeros(g * 64), src, zeros(64 - g * 64)]
    cols += [seg["f_k"], seg["f_v"]]
    win = jnp.concatenate([c for c in cols if c.shape[-1] > 0], axis=-1).astype(BF16)

    uq = mla_w_uq.reshape(L, MLA_Q_LORA, MLA_HEADS, MLA_NOPE + MLA_ROPE)
    wq_nope = uq[..., :MLA_NOPE].reshape(L, MLA_Q_LORA, MLA_HEADS * MLA_NOPE).astype(BF16)
    pe = uq[..., MLA_NOPE:]
    wq_pe = pe.reshape(L, MLA_Q_LORA, MLA_HEADS * MLA_ROPE).astype(BF16)
    wq_pe_sw = jnp.concatenate([pe[..., 16:], pe[..., :16]], axis=-1).reshape(
        L, MLA_Q_LORA, MLA_HEADS * MLA_ROPE).astype(BF16)
    eye = jnp.eye(MLA_HEADS, dtype=F32)
    wuk_bd = jnp.einsum("lchn,hk->lhnkc", mla_w_uk, eye).reshape(L, MLA_HEADS * MLA_NOPE,
                                                               MLA_HEADS * MLA_KV_LORA).astype(BF16)
    eye2 = jnp.eye(2, dtype=F32)
    uv = mla_w_uv.reshape(L, MLA_KV_LORA, 4, 2, MLA_V)
    wuv_bd = jnp.einsum("lcjev,ef->ljecfv", uv, eye2).reshape(L, 4, 256, 128).astype(BF16)

    w_a = ffn_w_up[:, :, :D_FF].reshape(L, D_MODEL, FFN_NCHUNK, FFN_CHUNK).transpose(0, 2, 1, 3).astype(BF16)
    w_g = ffn_w_up[:, :, D_FF:].reshape(L, D_MODEL, FFN_NCHUNK, FFN_CHUNK).transpose(0, 2, 1, 3).astype(BF16)
    w_down = ffn_w_down.reshape(L, FFN_NCHUNK, FFN_CHUNK, D_MODEL).astype(BF16)
    conv = jnp.concatenate([ffn_conv_w, ffn_conv_b[:, None, :], jnp.zeros((L, 4, D_FF), F32)], axis=1)
    conv = conv.reshape(L, 8, FFN_NCHUNK, FFN_CHUNK).transpose(0, 2, 1, 3)
    return dict(win=win, wq_nope=wq_nope, wq_pe=wq_pe, wq_pe_sw=wq_pe_sw, wuk_bd=wuk_bd, wuv_bd=wuv_bd,
                w_o=w_o.astype(BF16), w_a=w_a, w_g=w_g, w_down=w_down, conv=conv)


def _rope_tables(pos):
    half = MLA_ROPE // 2
    freqs = ROPE_THETA ** (-jnp.arange(half, dtype=F32) / half)
    ang = pos.astype(F32)[:, None] * freqs
    cos, sin = jnp.cos(ang), jnp.sin(ang)
    ck = jnp.concatenate([cos, cos], axis=-1)
    sk = jnp.concatenate([-sin, sin], axis=-1)
    return jnp.tile(ck, (1, MLA_HEADS)), jnp.tile(sk, (1, MLA_HEADS)), ck, sk


def kernel(x_prompt, x_sample, c_prompt, c_sample, cache_mla, cache_diff_kv, cache_fox_kv, cache_fox_logf, state_conv, page_table, w_ada, b_ada, g_attn_pre, g_attn_post, g_ffn_pre, g_ffn_post, w_in, mla_q_norm, mla_kv_norm, mla_w_uq, mla_w_uk, mla_w_uv, diff_lambda, diff_subln, fox_b_f, w_o, ffn_w_up, ffn_conv_w, ffn_conv_b, ffn_w_down):
    L = N_LAYERS
    B, T, _ = x_prompt.shape
    DB = x_sample.shape[0]
    n_pool = cache_mla.shape[1]
    past_len = page_table.shape[1] * PAGE

    wts = _prep_weights(w_in, mla_w_uq, mla_w_uk, mla_w_uv, w_o, ffn_w_up, ffn_conv_w, ffn_conv_b, ffn_w_down)
    tabs_p = _rope_tables(jnp.arange(T, dtype=jnp.int32))
    tabs_s = _rope_tables(jnp.full((DB,), past_len, jnp.int32))
    tri = (jnp.arange(T)[:, None] <= jnp.arange(T)[None, :]).astype(BF16)
    triu = (jnp.arange(PAGE)[:, None] > jnp.arange(PAGE)[None, :]).astype(BF16)

    cache_mla_t = jnp.transpose(cache_mla, (0, 1, 3, 2))
    cache_diff_t = jnp.transpose(cache_diff_kv, (0, 1, 3, 4, 5, 2)).reshape(L, n_pool, 256, PAGE)
    cache_fox_t = jnp.transpose(cache_fox_kv, (0, 1, 3, 4, 5, 2)).reshape(L, n_pool, 256, PAGE)
    cache_logf_t = jnp.pad(jnp.transpose(cache_fox_logf, (0, 1, 3, 2)), ((0, 0), (0, 0), (0, 4), (0, 0)))
    caches = (cache_mla_t, cache_diff_t, cache_fox_t, cache_logf_t)

    mod_all = _ada_call(jnp.concatenate([c_prompt, c_sample], axis=0), w_ada, b_ada)

    xp = x_prompt
    xs = x_sample.reshape(1, DB, D_MODEL)
    outs = [[] for _ in range(10)]
    for l in range(L):
        lam_init = 0.8 - 0.6 * math.exp(-0.3 * l)
        lw = {k: v[l] for k, v in wts.items()}
        lw.update(g_attn_pre=g_attn_pre[l][None], g_attn_post=g_attn_post[l][None], g_ffn_pre=g_ffn_pre[l][None],
                  g_ffn_post=g_ffn_post[l][None], mla_q_norm=mla_q_norm[l][None], mla_kv_norm=mla_kv_norm[l][None],
                  fox_b_f=fox_b_f[l][None])
        lam_p = diff_lambda[l]
        subln = diff_subln[l].reshape(1, DIFF_HEADS * HEAD_DIM)
        mod_p = mod_all[l, :B].reshape(B, 1, 6 * D_MODEL)
        mod_s = mod_all[l, B:].reshape(1, DB, 6 * D_MODEL)

        mq, mrow, dq, drow, fq, frow, logf = _pre_call(xp, mod_p, lw, tabs_p, PRE_TM)
        logf_t = jnp.pad(jnp.transpose(logf, (0, 2, 1)), ((0, 0), (0, 4), (0, 0)))
        f_cum = _cumsum_call(logf_t, tri)
        f_cum_t = jnp.transpose(f_cum[:, :FOX_HEADS], (0, 2, 1))
        f_cum_k = f_cum.reshape(B, 8, T // ATT_TK, ATT_TK).transpose(0, 2, 1, 3)
        olat, diff_o, fox_o = _attn_call(mq, dq, fq, mrow, drow, frow, f_cum_k, f_cum_t, lam_p, subln, lam_init)
        xp, tail = _post_call(xp, olat, diff_o, fox_o, mod_p, lw, POST_TM)
        conv_p = tail[:, :, 6:8, :].transpose(0, 2, 1, 3).reshape(B, 2, D_FF)

        mq_s, mrow_s, dq_s, drow_s, fq_s, frow_s, logf_s = _pre_call(xs, mod_s, lw, tabs_s, DB)
        fq_s8 = jnp.pad(jnp.transpose(fq_s[0], (1, 0, 2)), ((0, 0), (0, 4), (0, 0)))
        lf_s8 = jnp.pad(logf_s[0], ((0, 0), (0, 4)))[:, :, None]
        m_o, d_o, f_o = _decode_call(
            page_table, l, caches, jnp.transpose(mq_s[0], (1, 0, 2)), jnp.transpose(dq_s[0], (1, 0, 2)), fq_s8,
            mrow_s.reshape(DB, 1, MLA_ROW), drow_s.reshape(DB, 1, 256), frow_s.reshape(DB, 1, 256), lf_s8, triu)
        diff_s, fox_s = _decode_finish_call(jnp.transpose(d_o, (1, 0, 2)), jnp.transpose(f_o[:, :4], (1, 0, 2)),
                                            lam_p, subln, lam_init)
        olat_s = m_o.reshape(1, DB, MLA_HEADS * 128).astype(BF16)
        st = state_conv[l]
        xs, a_s = _post_call(xs, olat_s, diff_s[None], fox_s[None], mod_s, lw, DB, state=(st[:, 0], st[:, 1]))
        conv_s = jnp.stack([st[:, 1], a_s.transpose(1, 0, 2).reshape(DB, D_FF)], axis=1)

        for i, v in enumerate((mrow, mrow_s.reshape(DB, 1, MLA_ROW), drow.reshape(B, T, 2, 2, HEAD_DIM),
                               drow_s.reshape(DB, 1, 2, 2, HEAD_DIM), frow.reshape(B, T, 2, 2, HEAD_DIM),
                               frow_s.reshape(DB, 1, 2, 2, HEAD_DIM), logf, logf_s.reshape(DB, 1, FOX_HEADS),
                               conv_p, conv_s)):
            outs[i].append(v)

    return (xp, xs.reshape(DB, 1, D_MODEL)) + tuple(jnp.stack(o) for o in outs)
```

```python
import functools
import math

import jax
import jax.numpy as jnp
import numpy as np
from jax import lax
from jax.experimental import pallas as pl
from jax.experimental.pallas import tpu as pltpu

F32 = jnp.float32
BF16 = jnp.bfloat16

D_MODEL = 1024
D_FF = 2816
N_LAYERS = 4
MLA_HEADS = 8
MLA_Q_LORA = 256
MLA_KV_LORA = 128
MLA_NOPE = 64
MLA_ROPE = 32
MLA_V = 64
MLA_ROW = MLA_KV_LORA + MLA_ROPE
DIFF_HEADS = 4
DIFF_HALF = 32
FOX_HEADS = 4
HEAD_DIM = 64
ROPE_THETA = 10000.0
RMS_EPS = 1e-6
NEG_INF = -1e30
PAGE = 128

C_QLAT = 0
C_KVLAT = 256
C_KR = 384
C_KRS = 512
C_FZ = 640
C_DQ = 768
C_DROW = 1792
C_FQ = 2048
C_FROW = 2560
C_TOTAL = 2816

FFN_CHUNK = 256
FFN_NCHUNK = D_FF // FFN_CHUNK

PRE_TM = 512
POST_TM = 512
ATT_TQ = 128
ATT_TK = 512
DEC_PAGES = 32
LOG2E = math.log2(math.e)
MLA_QSCALE = LOG2E / math.sqrt(MLA_NOPE + MLA_ROPE)
DIFF_QSCALE = LOG2E / math.sqrt(DIFF_HALF)
FOX_QSCALE = LOG2E / math.sqrt(HEAD_DIM)
VMEM_LIMIT = 56 * 1024 * 1024


def _cparams(sem):
    return pltpu.CompilerParams(dimension_semantics=sem, vmem_limit_bytes=VMEM_LIMIT)


def _const_spec(shape, index_map):
    return pl.BlockSpec(shape, index_map, pipeline_mode=pl.Buffered(1))


def _rms(x, g):
    return x * lax.rsqrt(jnp.mean(x * x, axis=-1, keepdims=True) + RMS_EPS) * g


def _dot(a, b):
    return jnp.dot(a, b, preferred_element_type=F32)


def _dot_nt(a, b):
    return lax.dot_general(a, b, (((1,), (1,)), ((), ())), preferred_element_type=F32)


def _ada_kernel(c_ref, w_ref, b_ref, o_ref):
    c = c_ref[...]
    s = (c * jax.nn.sigmoid(c)).astype(BF16)
    o_ref[...] = _dot(s, w_ref[...].astype(BF16)) + b_ref[...]


def _ada_call(c_all, w_ada, b_ada):
    n = c_all.shape[0]
    tn = 1536
    return pl.pallas_call(
        _ada_kernel,
        grid=(N_LAYERS, 6 * D_MODEL // tn),
        in_specs=[pl.BlockSpec((n, D_MODEL), lambda l, j: (0, 0)),
                  pl.BlockSpec((None, D_MODEL, tn), lambda l, j: (l, 0, j)),
                  pl.BlockSpec((None, 1, tn), lambda l, j: (l, 0, j))],
        out_specs=pl.BlockSpec((None, n, tn), lambda l, j: (l, 0, j)),
        out_shape=jax.ShapeDtypeStruct((N_LAYERS, n, 6 * D_MODEL), F32),
        compiler_params=_cparams(("arbitrary", "arbitrary")),
        name="ada",
    )(c_all, w_ada, b_ada.reshape(N_LAYERS, 1, 6 * D_MODEL))


def _pre_kernel(x_ref, shift_ref, scale_ref, g_ref, win_ref, qn_ref, kvn_ref, wqn_ref, wqp_ref, wqps_ref,
                wuk_ref, bf_ref, cq_ref, sq_ref, ck_ref, sk_ref,
                mq_ref, mrow_ref, dq_ref, drow_ref, fq_ref, frow_ref, logf_ref):
    x = x_ref[...]
    h = _rms(x, g_ref[...]) * (1.0 + scale_ref[...]) + shift_ref[...]
    z = _dot(h.astype(BF16), win_ref[...])

    qn = _rms(z[:, C_QLAT:C_QLAT + MLA_Q_LORA], qn_ref[...]).astype(BF16)
    q_nope = _dot(qn, wqn_ref[...])
    q_pe = _dot(qn, wqp_ref[...]) * cq_ref[...] + _dot(qn, wqps_ref[...]) * sq_ref[...]
    q_abs = _dot(q_nope.astype(BF16), wuk_ref[...])
    for hh in range(MLA_HEADS):
        mq_ref[hh, :, 0:MLA_KV_LORA] = (q_abs[:, hh * 128:(hh + 1) * 128] * MLA_QSCALE).astype(BF16)
        mq_ref[hh, :, MLA_KV_LORA:MLA_ROW] = (q_pe[:, hh * MLA_ROPE:(hh + 1) * MLA_ROPE] * MLA_QSCALE).astype(BF16)

    mrow_ref[:, 0:MLA_KV_LORA] = _rms(z[:, C_KVLAT:C_KVLAT + MLA_KV_LORA], kvn_ref[...])
    mrow_ref[:, MLA_KV_LORA:MLA_ROW] = (z[:, C_KR:C_KR + MLA_ROPE] * ck_ref[...]
                                        + z[:, C_KRS:C_KRS + MLA_ROPE] * sk_ref[...])

    for j in range(8):
        dq_ref[j] = (z[:, C_DQ + j * 128:C_DQ + (j + 1) * 128] * DIFF_QSCALE).astype(BF16)
    drow_ref[...] = z[:, C_DROW:C_DROW + 256]
    for j in range(4):
        fq_ref[j] = (z[:, C_FQ + j * 128:C_FQ + (j + 1) * 128] * FOX_QSCALE).astype(BF16)
    frow_ref[...] = z[:, C_FROW:C_FROW + 256]

    fz = z[:, C_FZ:C_FZ + FOX_HEADS] + bf_ref[...]
    logf_ref[...] = jnp.minimum(fz, 0.0) - jnp.log1p(jnp.exp(-jnp.abs(fz)))


def _pre_call(x, mod, lw, tabs, tm):
    B, T, _ = x.shape
    R = mod.shape[1]
    rb = 1 if R == 1 else tm
    mod_map = (lambda c: (lambda b, t: (b, 0, c))) if R == 1 else (lambda c: (lambda b, t: (b, t, c)))
    cq, sq, ck, sk = tabs
    w2 = lambda shape: _const_spec(shape, lambda b, t: (0, 0))
    row = lambda n: pl.BlockSpec((tm, n), lambda b, t: (t, 0))
    outs = pl.pallas_call(
        _pre_kernel,
        grid=(B, T // tm),
        in_specs=[pl.BlockSpec((None, tm, D_MODEL), lambda b, t: (b, t, 0)),
                  pl.BlockSpec((None, rb, D_MODEL), mod_map(0)),
                  pl.BlockSpec((None, rb, D_MODEL), mod_map(1)),
                  w2((1, D_MODEL)), w2((D_MODEL, C_TOTAL)), w2((1, MLA_Q_LORA)), w2((1, MLA_KV_LORA)),
                  w2((MLA_Q_LORA, 512)), w2((MLA_Q_LORA, 256)), w2((MLA_Q_LORA, 256)), w2((512, 1024)),
                  w2((1, FOX_HEADS)), row(256), row(256), row(MLA_ROPE), row(MLA_ROPE)],
        out_specs=[pl.BlockSpec((None, 8, tm, MLA_ROW), lambda b, t: (b, 0, t, 0)),
                   pl.BlockSpec((None, tm, MLA_ROW), lambda b, t: (b, t, 0)),
                   pl.BlockSpec((None, 8, tm, 128), lambda b, t: (b, 0, t, 0)),
                   pl.BlockSpec((None, tm, 256), lambda b, t: (b, t, 0)),
                   pl.BlockSpec((None, 4, tm, 128), lambda b, t: (b, 0, t, 0)),
                   pl.BlockSpec((None, tm, 256), lambda b, t: (b, t, 0)),
                   pl.BlockSpec((None, tm, FOX_HEADS), lambda b, t: (b, t, 0))],
        out_shape=[jax.ShapeDtypeStruct((B, 8, T, MLA_ROW), BF16),
                   jax.ShapeDtypeStruct((B, T, MLA_ROW), F32),
                   jax.ShapeDtypeStruct((B, 8, T, 128), BF16),
                   jax.ShapeDtypeStruct((B, T, 256), F32),
                   jax.ShapeDtypeStruct((B, 4, T, 128), BF16),
                   jax.ShapeDtypeStruct((B, T, 256), F32),
                   jax.ShapeDtypeStruct((B, T, FOX_HEADS), F32)],
        compiler_params=_cparams(("arbitrary", "arbitrary")),
        name="pre",
    )(x, mod, mod, lw["g_attn_pre"], lw["win"], lw["mla_q_norm"], lw["mla_kv_norm"], lw["wq_nope"],
      lw["wq_pe"], lw["wq_pe_sw"], lw["wuk_bd"], lw["fox_b_f"], cq, sq, ck, sk)
    return outs


def _diag_mask(nblk, tq):
    rows = lax.broadcasted_iota(jnp.int32, (nblk * tq, tq), 0) & (tq - 1)
    cols = lax.broadcasted_iota(jnp.int32, (nblk * tq, tq), 1)
    return cols <= rows


def _alibi_slope(head):
    return jnp.where(head == 0, 2.0 ** -2, jnp.where(head == 1, 2.0 ** -4,
                                                     jnp.where(head == 2, 2.0 ** -6, 2.0 ** -8))).astype(F32)


def _lambda_value(lam_ref, lam_init):
    lv = lam_ref[...]
    a = jnp.sum(lv[0:1, :] * lv[1:2, :], axis=-1, keepdims=True)
    b = jnp.sum(lv[2:3, :] * lv[3:4, :], axis=-1, keepdims=True)
    return jnp.exp(a) - jnp.exp(b) + lam_init


def _diff_combine(blocks, lam, subln, lam_init):
    outs = []
    for g in range(2):
        for r in range(2):
            o = blocks[g * 4 + r * 2] - lam * blocks[g * 4 + r * 2 + 1]
            o = o[:, g * HEAD_DIM:(g + 1) * HEAD_DIM]
            hh = g * 2 + r
            outs.append(_rms(o, subln[:, hh * HEAD_DIM:(hh + 1) * HEAD_DIM]) * (1.0 - lam_init))
    return jnp.concatenate(outs, axis=1)


def _fox_select(blocks):
    return jnp.concatenate([blocks[g * 2 + r][:, g * HEAD_DIM:(g + 1) * HEAD_DIM]
                            for g in range(2) for r in range(2)], axis=1)


def _softmax_step(chunks, v_bf, m_ref, l_ref, acc_ref):
    mx = chunks[0]
    for c in chunks[1:]:
        mx = jnp.maximum(mx, c)
    m_prev = m_ref[...]
    m_new = jnp.maximum(m_prev, jnp.max(mx, axis=-1, keepdims=True))
    alpha = jnp.exp2(m_prev - m_new)
    ps = [jnp.exp2(c - m_new) for c in chunks]
    lsum = ps[0]
    for p in ps[1:]:
        lsum = lsum + p
    l_ref[...] = alpha * l_ref[...] + lsum
    p_bf = jnp.concatenate([p.astype(BF16) for p in ps], axis=1) if len(ps) > 1 else ps[0].astype(BF16)
    acc_ref[...] = alpha * acc_ref[...] + _dot(p_bf, v_bf)
    m_ref[...] = m_new


def _attn_kernel(lam_init, mq_ref, dq_ref, fq_ref, mkv_ref, dkv_ref, fkv_ref, fk_ref, fqc_ref, lam_ref, subln_ref,
                 mo_ref, do_ref, fo_ref, mm, ml, macc, dm, dl, dacc, fm, fl, facc, sb_sc, sl_sc):
    tq, tk = ATT_TQ, ATT_TK
    per_tile = tk // tq
    qi = pl.program_id(1)

    @pl.when((pl.program_id(0) == 0) & (qi == 0))
    def _():
        rows = lax.broadcasted_iota(jnp.int32, (8 * tq, tk), 0)
        cols = lax.broadcasted_iota(jnp.int32, (8 * tq, tk), 1)
        slope = _alibi_slope(rows >> (int(math.log2(tq)) + 1)) * LOG2E
        sb_sc[...] = slope * ((rows & (tq - 1)) - cols).astype(F32)
        sl_sc[...] = slope[:, 0:128]

    for m_ref, l_ref, acc_ref in ((mm, ml, macc), (dm, dl, dacc), (fm, fl, facc)):
        m_ref[...] = jnp.full(m_ref.shape, NEG_INF, F32)
        l_ref[...] = jnp.zeros(l_ref.shape, F32)
        acc_ref[...] = jnp.zeros(acc_ref.shape, F32)

    mq = mq_ref[...].reshape(MLA_HEADS * tq, MLA_ROW)
    dq = dq_ref[...].reshape(8 * tq, 128)
    fq = fq_ref[...].reshape(4 * tq, 128)
    fqc = fqc_ref[...] * LOG2E
    fq_bias = jnp.concatenate([jnp.broadcast_to(fqc[:, hh:hh + 1], (tq, 128)) for hh in range(FOX_HEADS)], axis=0)

    def tile(kt, width, diag, off):
        nc = width // 128
        k0 = pl.multiple_of(kt * tk, tk)
        sl = lambda s, j: s[:, j * 128:(j + 1) * 128]

        kb = mkv_ref[pl.ds(k0, width), :].astype(BF16)
        s = _dot_nt(mq, kb)
        chunks = [sl(s, j) for j in range(nc)]
        if diag:
            chunks[-1] = jnp.where(_diag_mask(MLA_HEADS, tq), chunks[-1], NEG_INF)
        _softmax_step(chunks, kb[:, 0:MLA_KV_LORA], mm, ml, macc)

        kb = dkv_ref[pl.ds(k0, width), 0:128].astype(BF16)
        vb = dkv_ref[pl.ds(k0, width), 128:256].astype(BF16)
        s = _dot_nt(dq, kb)
        slope_off = sl_sc[...] * off
        chunks = [(sl(s, j) - sb_sc[:, j * 128:(j + 1) * 128]) - slope_off for j in range(nc)]
        if diag:
            chunks[-1] = jnp.where(_diag_mask(8, tq), chunks[-1], NEG_INF)
        _softmax_step(chunks, vb, dm, dl, dacc)

        kb = fkv_ref[pl.ds(k0, width), 0:128].astype(BF16)
        vb = fkv_ref[pl.ds(k0, width), 128:256].astype(BF16)
        s = _dot_nt(fq, kb)
        fk = fk_ref[kt] * LOG2E
        chunks = []
        for j in range(nc):
            fk_j = jnp.concatenate([jnp.broadcast_to(fk[hh:hh + 1, j * 128:(j + 1) * 128], (tq, 128))
                                    for hh in range(FOX_HEADS)], axis=0)
            chunks.append((sl(s, j) + fq_bias) - fk_j)
        if diag:
            chunks[-1] = jnp.where(_diag_mask(4, tq), chunks[-1], NEG_INF)
        _softmax_step(chunks, vb, fm, fl, facc)

    n_full = qi // per_tile
    rem = qi - n_full * per_tile

    def body(kt, c):
        tile(kt, tk, False, (qi * tq - kt * tk).astype(F32))
        return c

    lax.fori_loop(0, n_full, body, 0)
    for r in range(per_tile):
        @pl.when(rem == r)
        def _():
            tile(n_full, (r + 1) * tq, True, float(r * tq))

    def finish(l_ref, acc_ref):
        return acc_ref[...] / jnp.sum(l_ref[...], axis=-1, keepdims=True)

    o = finish(ml, macc)
    for hh in range(MLA_HEADS):
        mo_ref[:, hh * 128:(hh + 1) * 128] = o[hh * tq:(hh + 1) * tq].astype(BF16)
    o = finish(dl, dacc)
    lam = _lambda_value(lam_ref, lam_init)
    do_ref[...] = _diff_combine([o[j * tq:(j + 1) * tq] for j in range(8)], lam, subln_ref[...],
                                lam_init).astype(BF16)
    o = finish(fl, facc)
    fo_ref[...] = _fox_select([o[j * tq:(j + 1) * tq] for j in range(4)]).astype(BF16)


def _attn_call(mq, dq, fq, mrow, drow, frow, f_cum_k, f_cum_t, lam_p, subln, lam_init):
    B, _, T, _ = mq.shape
    tq, tk = ATT_TQ, ATT_TK
    qblk = lambda n, d: pl.BlockSpec((None, n, tq, d), lambda b, i: (b, 0, i, 0))
    full = lambda d: pl.BlockSpec((None, T, d), lambda b, i: (b, 0, 0))
    oblk = lambda d: pl.BlockSpec((None, tq, d), lambda b, i: (b, i, 0))
    stat = lambda n: [pltpu.VMEM((n * tq, 128), F32)] * 3
    return pl.pallas_call(
        functools.partial(_attn_kernel, lam_init),
        grid=(B, T // tq),
        in_specs=[qblk(MLA_HEADS, MLA_ROW), qblk(8, 128), qblk(4, 128), full(MLA_ROW), full(256), full(256),
                  pl.BlockSpec((None, T // tk, 8, tk), lambda b, i: (b, 0, 0, 0)),
                  pl.BlockSpec((None, tq, FOX_HEADS), lambda b, i: (b, i, 0)),
                  pl.BlockSpec((4, DIFF_HALF), lambda b, i: (0, 0)),
                  pl.BlockSpec((1, 256), lambda b, i: (0, 0))],
        out_specs=[oblk(MLA_HEADS * 128), oblk(256), oblk(256)],
        out_shape=[jax.ShapeDtypeStruct((B, T, MLA_HEADS * 128), BF16),
                   jax.ShapeDtypeStruct((B, T, 256), BF16), jax.ShapeDtypeStruct((B, T, 256), BF16)],
        scratch_shapes=stat(MLA_HEADS) + stat(8) + stat(4) + [pltpu.VMEM((8 * tq, tk), F32),
                                                              pltpu.VMEM((8 * tq, 128), F32)],
        compiler_params=_cparams(("arbitrary", "arbitrary")),
        name="prefill_attn",
    )(mq, dq, fq, mrow, drow, frow, f_cum_k, f_cum_t, lam_p, subln)


def _split3(x):
    hi = x.astype(BF16)
    r = x - hi.astype(F32)
    mid = r.astype(BF16)
    lo = (r - mid.astype(F32)).astype(BF16)
    return hi, mid, lo


def _cumsum_kernel(x_ref, tri_ref, o_ref):
    hi, mid, lo = _split3(x_ref[...])
    tri = tri_ref[...]
    o_ref[...] = (_dot(hi, tri) + _dot(mid, tri)) + _dot(lo, tri)


def _cumsum_call(x, tri):
    B, R, T = x.shape
    return pl.pallas_call(
        _cumsum_kernel,
        grid=(B,),
        in_specs=[pl.BlockSpec((None, R, T), lambda b: (b, 0, 0)),
                  _const_spec((T, T), lambda b: (0, 0))],
        out_specs=pl.BlockSpec((None, R, T), lambda b: (b, 0, 0)),
        out_shape=jax.ShapeDtypeStruct((B, R, T), F32),
        compiler_params=_cparams(("arbitrary",)),
        name="logf_cumsum",
    )(x, tri)


def _decode_kernel(past_len, layer, nseq, nch, pt_ref, mq_ref, dq_ref, fq_ref, mnew_ref, dnew_ref, fnew_ref,
                   lfnew_ref, triu_ref, mla_hbm, diff_hbm, fox_hbm, logf_hbm, mo_ref, do_ref, fo_ref,
                   mla_buf, diff_buf, fox_buf, logf_buf, sem, mm, ml, macc, dm, dl, dacc, fm, fl, facc, carry):
    np_ = DEC_PAGES
    b = pl.program_id(0)
    c = pl.program_id(1)
    width = np_ * PAGE
    step = b * nch + c
    slot = step % 2

    def page_copies(bb, cc, sl, j):
        page = pt_ref[bb, (nch - 1 - cc) * np_ + j]
        pairs = ((mla_hbm, mla_buf), (diff_hbm, diff_buf), (fox_hbm, fox_buf), (logf_hbm, logf_buf))
        return [pltpu.make_async_copy(hbm.at[layer, page], buf.at[sl, j], sem.at[sl, k])
                for k, (hbm, buf) in enumerate(pairs)]

    def start_step(bb, cc, sl):
        def body(j, carry_):
            for cp in page_copies(bb, cc, sl, j):
                cp.start()
            return carry_
        lax.fori_loop(0, np_, body, 0)

    @pl.when(step == 0)
    def _():
        start_step(b, c, slot)

    nxt = step + 1

    @pl.when(nxt < nseq * nch)
    def _():
        start_step(nxt // nch, nxt % nch, 1 - slot)

    def wait_body(j, carry_):
        for cp in page_copies(b, c, slot, j):
            cp.wait()
        return carry_
    lax.fori_loop(0, np_, wait_body, 0)

    mla_pages = [mla_buf.at[slot, j] for j in range(np_)]
    diff_pages = [diff_buf.at[slot, j] for j in range(np_)]
    fox_pages = [fox_buf.at[slot, j] for j in range(np_)]
    logf_pages = [logf_buf.at[slot, j] for j in range(np_)]

    mq = mq_ref[...]
    dq = dq_ref[...]
    fq = fq_ref[...]

    @pl.when(c == 0)
    def _():
        def self_init(q, k_new, v_new, m_sc, l_sc, acc_sc):
            kb = k_new.astype(BF16).astype(F32)
            m_sc[...] = jnp.sum(q.astype(F32) * kb, axis=-1, keepdims=True)
            l_sc[...] = jnp.ones(l_sc.shape, F32)
            acc_sc[...] = jnp.broadcast_to(v_new.astype(BF16).astype(F32), acc_sc.shape)

        mnew = mnew_ref[...]
        self_init(mq, mnew, mnew[:, 0:MLA_KV_LORA], mm, ml, macc)
        dnew = dnew_ref[...]
        self_init(dq, dnew[:, 0:128], dnew[:, 128:256], dm, dl, dacc)
        fnew = fnew_ref[...]
        self_init(fq, fnew[:, 0:128], fnew[:, 128:256], fm, fl, facc)
        carry[...] = lfnew_ref[...] * LOG2E

    def update(s, vt_bf, m_sc, l_sc, acc_sc):
        m_prev = m_sc[...]
        m_new = jnp.maximum(m_prev, jnp.max(s, axis=-1, keepdims=True))
        alpha = jnp.exp2(m_prev - m_new)
        p = jnp.exp2(s - m_new)
        l_sc[...] = alpha * l_sc[...] + jnp.sum(p, axis=-1, keepdims=True)
        acc_sc[...] = alpha * acc_sc[...] + _dot_nt(p.astype(BF16), vt_bf)
        m_sc[...] = m_new

    mla_t = jnp.concatenate([r[...].astype(BF16) for r in mla_pages], axis=1)
    update(_dot(mq, mla_t), mla_t[0:MLA_KV_LORA, :], mm, ml, macc)

    first_pos = (nch - 1 - c) * width
    diff_k = jnp.concatenate([r[0:128, :].astype(BF16) for r in diff_pages], axis=1)
    diff_v = jnp.concatenate([r[128:256, :].astype(BF16) for r in diff_pages], axis=1)
    kpos = first_pos + lax.broadcasted_iota(jnp.int32, (8, width), 1)
    slope = _alibi_slope(lax.broadcasted_iota(jnp.int32, (8, width), 0) >> 1) * LOG2E
    s_d = _dot(dq, diff_k) - slope * (past_len - kpos).astype(F32)
    update(s_d, diff_v, dm, dl, dacc)

    fox_k = jnp.concatenate([r[0:128, :].astype(BF16) for r in fox_pages], axis=1)
    fox_v = jnp.concatenate([r[128:256, :].astype(BF16) for r in fox_pages], axis=1)
    lf = [r[...] * LOG2E for r in logf_pages]
    hi, mid, lo = _split3(jnp.concatenate(lf, axis=0))
    triu = triu_ref[...]
    within = (_dot(hi, triu) + _dot(mid, triu)) + _dot(lo, triu)
    run = carry[...]
    bias = [None] * np_
    for j in range(np_ - 1, -1, -1):
        bias[j] = within[j * 8:(j + 1) * 8] + run
        run = run + jnp.sum(lf[j], axis=-1, keepdims=True)
    carry[...] = run
    s_f = _dot(fq, fox_k) + jnp.concatenate(bias, axis=1)
    update(s_f, fox_v, fm, fl, facc)

    @pl.when(c == nch - 1)
    def _():
        mo_ref[...] = macc[...] / ml[...]
        do_ref[...] = dacc[...] / dl[...]
        fo_ref[...] = facc[...] / fl[...]


def _decode_call(page_table, layer, caches, mq, dq, fq, mnew, dnew, fnew, lfnew, triu):
    cache_mla_t, cache_diff_t, cache_fox_t, cache_logf_t = caches
    nseq, npages = page_table.shape
    past_len = npages * PAGE
    nch = npages // DEC_PAGES

    seq3 = lambda r, n: pl.BlockSpec((None, r, n), lambda b, c, pt: (b, 0, 0))
    in_specs = [seq3(8, MLA_ROW), seq3(8, 128), seq3(8, 128), seq3(1, MLA_ROW), seq3(1, 256), seq3(1, 256),
                seq3(8, 1), pl.BlockSpec((PAGE, PAGE), lambda b, c, pt: (0, 0))]
    in_specs += [pl.BlockSpec(memory_space=pl.ANY)] * 4
    small = lambda: [pltpu.VMEM((8, 1), F32), pltpu.VMEM((8, 1), F32), pltpu.VMEM((8, 128), F32)]
    page_bufs = [pltpu.VMEM((2, DEC_PAGES, rows, PAGE), F32) for rows in (MLA_ROW, 256, 256, 8)]
    grid_spec = pltpu.PrefetchScalarGridSpec(
        num_scalar_prefetch=1,
        grid=(nseq, nch),
        in_specs=in_specs,
        out_specs=[seq3(8, 128), seq3(8, 128), seq3(8, 128)],
        scratch_shapes=page_bufs + [pltpu.SemaphoreType.DMA((2, 4))] + small() + small() + small()
        + [pltpu.VMEM((8, 1), F32)],
    )
    return pl.pallas_call(
        functools.partial(_decode_kernel, past_len, layer, nseq, nch),
        grid_spec=grid_spec,
        out_shape=[jax.ShapeDtypeStruct((nseq, 8, 128), F32)] * 3,
        compiler_params=_cparams(("arbitrary", "arbitrary")),
        name="decode_attn",
    )(page_table, mq, dq, fq, mnew, dnew, fnew, lfnew, triu,
      cache_mla_t, cache_diff_t, cache_fox_t, cache_logf_t)


def _decode_finish_kernel(lam_init, d_ref, f_ref, lam_ref, subln_ref, do_ref, fo_ref):
    lam = _lambda_value(lam_ref, lam_init)
    do_ref[...] = _diff_combine([d_ref[j] for j in range(8)], lam, subln_ref[...], lam_init).astype(BF16)
    fo_ref[...] = _fox_select([f_ref[j] for j in range(4)]).astype(BF16)


def _decode_finish_call(d_blocks, f_blocks, lam_p, subln, lam_init):
    n = d_blocks.shape[1]
    return pl.pallas_call(
        functools.partial(_decode_finish_kernel, lam_init),
        out_shape=[jax.ShapeDtypeStruct((n, 256), BF16)] * 2,
        name="decode_finish",
    )(d_blocks, f_blocks, lam_p, subln)


def _post_kernel(decode, x_ref, olat_ref, diff_ref, fox_ref, gate_a_ref, shift_f_ref, scale_f_ref, gate_f_ref,
                 g_post_ref, g_fpre_ref, g_fpost_ref, wuv_ref, wo_ref, wa_ref, wg_ref, cw_ref, wd_ref, *rest):
    if decode:
        s0_ref, s1_ref, xo_ref, a_out_ref, y_sc = rest
    else:
        xo_ref, tail_ref, a_sc, y_sc = rest
    tm = x_ref.shape[0]
    olat = olat_ref[...]
    mla = [_dot(olat[:, j * 256:(j + 1) * 256], wuv_ref[j]).astype(BF16) for j in range(4)]
    att = jnp.concatenate(mla + [diff_ref[...], fox_ref[...]], axis=1)
    mix = _dot(att, wo_ref[...])
    x1 = x_ref[...] + gate_a_ref[...] * _rms(mix, g_post_ref[...])
    h = (_rms(x1, g_fpre_ref[...]) * (1.0 + scale_f_ref[...]) + shift_f_ref[...]).astype(BF16)

    if not decode:
        @pl.when(pl.program_id(1) == 0)
        def _():
            a_sc[:, 0:8, :] = jnp.zeros((FFN_NCHUNK, 8, FFN_CHUNK), F32)
    y_sc[...] = jnp.zeros(y_sc.shape, F32)

    def chunk(ci, carry):
        a = _dot(h, wa_ref[ci])
        g = _dot(h, wg_ref[ci])
        cw = cw_ref[ci]
        if decode:
            a_out_ref[ci] = a
            conv = cw[3:4] + cw[0:1] * s0_ref[ci] + cw[1:2] * s1_ref[ci] + cw[2:3] * a
        else:
            a_sc[ci, 8:8 + tm, :] = a
            conv = (cw[3:4] + cw[0:1] * a_sc[ci, 6:6 + tm, :] + cw[1:2] * a_sc[ci, 7:7 + tm, :] + cw[2:3] * a)
            last = a_sc[ci, tm:tm + 8, :]
            a_sc[ci, 0:8, :] = last
            tail_ref[ci] = last
        gelu = 0.5 * conv * (1.0 + jnp.tanh(math.sqrt(2.0 / math.pi) * (conv + 0.044715 * (conv * conv * conv))))
        y_sc[...] += _dot((gelu * g).astype(BF16), wd_ref[ci])
        return carry

    lax.fori_loop(0, FFN_NCHUNK, chunk, 0)
    xo_ref[...] = x1 + gate_f_ref[...] * _rms(y_sc[...], g_fpost_ref[...])


def _post_call(x, olat, diff_o, fox_o, mod, lw, tm, state=None):
    B, T, _ = x.shape
    R = mod.shape[1]
    rb = 1 if R == 1 else tm
    decode = state is not None
    mod_map = (lambda c: (lambda b, t: (b, 0, c))) if R == 1 else (lambda c: (lambda b, t: (b, t, c)))
    act = lambda n: pl.BlockSpec((None, tm, n), lambda b, t: (b, t, 0))
    w2 = lambda shape: _const_spec(shape, lambda b, t: (0, 0))
    w3 = lambda shape: _const_spec(shape, lambda b, t: (0, 0, 0))
    in_specs = [act(D_MODEL), act(MLA_HEADS * 128), act(256), act(256),
                pl.BlockSpec((None, rb, D_MODEL), mod_map(2)), pl.BlockSpec((None, rb, D_MODEL), mod_map(3)),
                pl.BlockSpec((None, rb, D_MODEL), mod_map(4)), pl.BlockSpec((None, rb, D_MODEL), mod_map(5)),
                w2((1, D_MODEL)), w2((1, D_MODEL)), w2((1, D_MODEL)),
                w3((4, 256, 128)), w2((D_MODEL, D_MODEL)),
                w3((FFN_NCHUNK, D_MODEL, FFN_CHUNK)), w3((FFN_NCHUNK, D_MODEL, FFN_CHUNK)),
                w3((FFN_NCHUNK, 8, FFN_CHUNK)), w3((FFN_NCHUNK, FFN_CHUNK, D_MODEL))]
    args = [x, olat, diff_o, fox_o, mod, mod, mod, mod, lw["g_attn_post"], lw["g_ffn_pre"], lw["g_ffn_post"],
            lw["wuv_bd"], lw["w_o"], lw["w_a"], lw["w_g"], lw["conv"], lw["w_down"]]
    if decode:
        chunked = pl.BlockSpec((FFN_NCHUNK, tm, FFN_CHUNK), lambda b, t: (0, t, 0))
        in_specs += [chunked] * 2
        args += [s.reshape(T, FFN_NCHUNK, FFN_CHUNK).transpose(1, 0, 2) for s in state]
        out_specs = [act(D_MODEL), chunked]
        out_shape = [jax.ShapeDtypeStruct((B, T, D_MODEL), F32),
                     jax.ShapeDtypeStruct((FFN_NCHUNK, T, FFN_CHUNK), F32)]
        scratch = [pltpu.VMEM((tm, D_MODEL), F32)]
    else:
        out_specs = [act(D_MODEL), pl.BlockSpec((None, FFN_NCHUNK, 8, FFN_CHUNK), lambda b, t: (b, 0, 0, 0))]
        out_shape = [jax.ShapeDtypeStruct((B, T, D_MODEL), F32),
                     jax.ShapeDtypeStruct((B, FFN_NCHUNK, 8, FFN_CHUNK), F32)]
        scratch = [pltpu.VMEM((FFN_NCHUNK, tm + 8, FFN_CHUNK), F32), pltpu.VMEM((tm, D_MODEL), F32)]
    return pl.pallas_call(
        functools.partial(_post_kernel, decode),
        grid=(B, T // tm),
        in_specs=in_specs,
        out_specs=out_specs,
        out_shape=out_shape,
        scratch_shapes=scratch,
        compiler_params=_cparams(("arbitrary", "arbitrary")),
        name="post_decode" if decode else "post",
    )(*args)


def _prep_weights(w_in, mla_w_uq, mla_w_uk, mla_w_uv, w_o, ffn_w_up, ffn_conv_w, ffn_conv_b, ffn_w_down):
    L = N_LAYERS
    zeros = lambda n: jnp.zeros((L, D_MODEL, n), F32)
    o = 0
    seg = {}
    for name, n in (("q_lat", 256), ("kv_lat", 128), ("k_r", 32), ("d_q", 256), ("d_k", 128), ("d_v", 128),
                    ("f_q", 256), ("f_k", 128), ("f_v", 128), ("f_z", 4)):
        seg[name] = w_in[:, :, o:o + n]
        o += n
    k_r = seg["k_r"]
    k_r_sw = jnp.concatenate([k_r[..., 16:], k_r[..., :16]], axis=-1)
    cols = [seg["q_lat"], seg["kv_lat"], k_r, zeros(96), k_r_sw, zeros(96), seg["f_z"], zeros(124)]
    for g in range(2):
        for r in range(2):
            for i in range(2):
                src = seg["d_q"][..., g * 128 + r * 64 + i * 32:g * 128 + r * 64 + (i + 1) * 32]
                lo = g * 64 + i * 32
                cols += [zeros(lo), src, zeros(128 - lo - 32)]
    cols += [seg["d_k"], seg["d_v"]]
    for g in range(2):
        for r in range(2):
            src = seg["f_q"][..., g * 128 + r * 64:g * 128 + (r + 1) * 64]
            cols += [zeros(g * 64), src, zeros(64 - g * 64)]
    cols += [seg["f_k"], seg["f_v"]]
    win = jnp.concatenate([c for c in cols if c.shape[-1] > 0], axis=-1).astype(BF16)

    uq = mla_w_uq.reshape(L, MLA_Q_LORA, MLA_HEADS, MLA_NOPE + MLA_ROPE)
    wq_nope = uq[..., :MLA_NOPE].reshape(L, MLA_Q_LORA, MLA_HEADS * MLA_NOPE).astype(BF16)
    pe = uq[..., MLA_NOPE:]
    wq_pe = pe.reshape(L, MLA_Q_LORA, MLA_HEADS * MLA_ROPE).astype(BF16)
    wq_pe_sw = jnp.concatenate([pe[..., 16:], pe[..., :16]], axis=-1).reshape(
        L, MLA_Q_LORA, MLA_HEADS * MLA_ROPE).astype(BF16)
    eye = jnp.eye(MLA_HEADS, dtype=F32)
    wuk_bd = jnp.einsum("lchn,hk->lhnkc", mla_w_uk, eye).reshape(L, MLA_HEADS * MLA_NOPE,
                                                               MLA_HEADS * MLA_KV_LORA).astype(BF16)
    eye2 = jnp.eye(2, dtype=F32)
    uv = mla_w_uv.reshape(L, MLA_KV_LORA, 4, 2, MLA_V)
    wuv_bd = jnp.einsum("lcjev,ef->ljecfv", uv, eye2).reshape(L, 4, 256, 128).astype(BF16)

    w_a = ffn_w_up[:, :, :D_FF].reshape(L, D_MODEL, FFN_NCHUNK, FFN_CHUNK).transpose(0, 2, 1, 3).astype(BF16)
    w_g = ffn_w_up[:, :, D_FF:].reshape(L, D_MODEL, FFN_NCHUNK, FFN_CHUNK).transpose(0, 2, 1, 3).astype(BF16)
    w_down = ffn_w_down.reshape(L, FFN_NCHUNK, FFN_CHUNK, D_MODEL).astype(BF16)
    conv = jnp.concatenate([ffn_conv_w, ffn_conv_b[:, None, :], jnp.zeros((L, 4, D_FF), F32)], axis=1)
    conv = conv.reshape(L, 8, FFN_NCHUNK, FFN_CHUNK).transpose(0, 2, 1, 3)
    return dict(win=win, wq_nope=wq_nope, wq_pe=wq_pe, wq_pe_sw=wq_pe_sw, wuk_bd=wuk_bd, wuv_bd=wuv_bd,
                w_o=w_o.astype(BF16), w_a=w_a, w_g=w_g, w_down=w_down, conv=conv)


def _rope_tables(pos):
    half = MLA_ROPE // 2
    freqs = ROPE_THETA ** (-jnp.arange(half, dtype=F32) / half)
    ang = pos.astype(F32)[:, None] * freqs
    cos, sin = jnp.cos(ang), jnp.sin(ang)
    ck = jnp.concatenate([cos, cos], axis=-1)
    sk = jnp.concatenate([-sin, sin], axis=-1)
    return jnp.tile(ck, (1, MLA_HEADS)), jnp.tile(sk, (1, MLA_HEADS)), ck, sk


def kernel(x_prompt, x_sample, c_prompt, c_sample, cache_mla, cache_diff_kv, cache_fox_kv, cache_fox_logf, state_conv, page_table, w_ada, b_ada, g_attn_pre, g_attn_post, g_ffn_pre, g_ffn_post, w_in, mla_q_norm, mla_kv_norm, mla_w_uq, mla_w_uk, mla_w_uv, diff_lambda, diff_subln, fox_b_f, w_o, ffn_w_up, ffn_conv_w, ffn_conv_b, ffn_w_down):
    L = N_LAYERS
    B, T, _ = x_prompt.shape
    DB = x_sample.shape[0]
    n_pool = cache_mla.shape[1]
    past_len = page_table.shape[1] * PAGE

    wts = _prep_weights(w_in, mla_w_uq, mla_w_uk, mla_w_uv, w_o, ffn_w_up, ffn_conv_w, ffn_conv_b, ffn_w_down)
    tabs_p = _rope_tables(jnp.arange(T, dtype=jnp.int32))
    tabs_s = _rope_tables(jnp.full((DB,), past_len, jnp.int32))
    tri = (jnp.arange(T)[:, None] <= jnp.arange(T)[None, :]).astype(BF16)
    triu = (jnp.arange(PAGE)[:, None] > jnp.arange(PAGE)[None, :]).astype(BF16)

    cache_mla_t = jnp.transpose(cache_mla, (0, 1, 3, 2))
    cache_diff_t = jnp.transpose(cache_diff_kv, (0, 1, 3, 4, 5, 2)).reshape(L, n_pool, 256, PAGE)
    cache_fox_t = jnp.transpose(cache_fox_kv, (0, 1, 3, 4, 5, 2)).reshape(L, n_pool, 256, PAGE)
    cache_logf_t = jnp.pad(jnp.transpose(cache_fox_logf, (0, 1, 3, 2)), ((0, 0), (0, 0), (0, 4), (0, 0)))
    caches = (cache_mla_t, cache_diff_t, cache_fox_t, cache_logf_t)

    mod_all = _ada_call(jnp.concatenate([c_prompt, c_sample], axis=0), w_ada, b_ada)

    xp = x_prompt
    xs = x_sample.reshape(1, DB, D_MODEL)
    outs = [[] for _ in range(10)]
    for l in range(L):
        lam_init = 0.8 - 0.6 * math.exp(-0.3 * l)
        lw = {k: v[l] for k, v in wts.items()}
        lw.update(g_attn_pre=g_attn_pre[l][None], g_attn_post=g_attn_post[l][None], g_ffn_pre=g_ffn_pre[l][None],
                  g_ffn_post=g_ffn_post[l][None], mla_q_norm=mla_q_norm[l][None], mla_kv_norm=mla_kv_norm[l][None],
                  fox_b_f=fox_b_f[l][None])
        lam_p = diff_lambda[l]
        subln = diff_subln[l].reshape(1, DIFF_HEADS * HEAD_DIM)
        mod_p = mod_all[l, :B].reshape(B, 1, 6 * D_MODEL)
        mod_s = mod_all[l, B:].reshape(1, DB, 6 * D_MODEL)

        mq, mrow, dq, drow, fq, frow, logf = _pre_call(xp, mod_p, lw, tabs_p, PRE_TM)
        logf_t = jnp.pad(jnp.transpose(logf, (0, 2, 1)), ((0, 0), (0, 4), (0, 0)))
        f_cum = _cumsum_call(logf_t, tri)
        f_cum_t = jnp.transpose(f_cum[:, :FOX_HEADS], (0, 2, 1))
        f_cum_k = f_cum.reshape(B, 8, T // ATT_TK, ATT_TK).transpose(0, 2, 1, 3)
        olat, diff_o, fox_o = _attn_call(mq, dq, fq, mrow, drow, frow, f_cum_k, f_cum_t, lam_p, subln, lam_init)
        xp, tail = _post_call(xp, olat, diff_o, fox_o, mod_p, lw, POST_TM)
        conv_p = tail[:, :, 6:8, :].transpose(0, 2, 1, 3).reshape(B, 2, D_FF)

        mq_s, mrow_s, dq_s, drow_s, fq_s, frow_s, logf_s = _pre_call(xs, mod_s, lw, tabs_s, DB)
        fq_s8 = jnp.pad(jnp.transpose(fq_s[0], (1, 0, 2)), ((0, 0), (0, 4), (0, 0)))
        lf_s8 = jnp.pad(logf_s[0], ((0, 0), (0, 4)))[:, :, None]
        m_o, d_o, f_o = _decode_call(
            page_table, l, caches, jnp.transpose(mq_s[0], (1, 0, 2)), jnp.transpose(dq_s[0], (1, 0, 2)), fq_s8,
            mrow_s.reshape(DB, 1, MLA_ROW), drow_s.reshape(DB, 1, 256), frow_s.reshape(DB, 1, 256), lf_s8, triu)
        diff_s, fox_s = _decode_finish_call(jnp.transpose(d_o, (1, 0, 2)), jnp.transpose(f_o[:, :4], (1, 0, 2)),
                                            lam_p, subln, lam_init)
        olat_s = m_o.reshape(1, DB, MLA_HEADS * 128).astype(BF16)
        st = state_conv[l]
        xs, a_s = _post_call(xs, olat_s, diff_s[None], fox_s[None], mod_s, lw, DB, state=(st[:, 0], st[:, 1]))
        conv_s = jnp.stack([st[:, 1], a_s.transpose(1, 0, 2).reshape(DB, D_FF)], axis=1)

        for i, v in enumerate((mrow, mrow_s.reshape(DB, 1, MLA_ROW), drow.reshape(B, T, 2, 2, HEAD_DIM),
                               drow_s.reshape(DB, 1, 2, 2, HEAD_DIM), frow.reshape(B, T, 2, 2, HEAD_DIM),
                               frow_s.reshape(DB, 1, 2, 2, HEAD_DIM), logf, logf_s.reshape(DB, 1, FOX_HEADS),
                               conv_p, conv_s)):
            outs[i].append(v)

    return (xp, xs.reshape(DB, 1, D_MODEL)) + tuple(jnp.stack(o) for o in outs)
```
